```python
import jax, jax.numpy as jnp
from jax import lax
import numpy as np

D_MODEL = 2048
BATCH = 1
SEQ = 16384
DEPTH = 1
DEC_BATCH = 16
DEC_SEQ = 2048
PAST_LEN = 128

CONV_DIM = D_MODEL // 2
RWKV_DIM = D_MODEL // 2
HEAD_SIZE = 64
N_RWKV_HEADS = RWKV_DIM // HEAD_SIZE
CONV_WIDTH = 3
DECAY_LORA = 64
AAA_LORA = 64
GATE_LORA = 160
D_FF = 5632
FFN_RESIDUAL_SCALE = 0.5
RMS_EPS = 1e-6
GN_EPS = 64e-5
L2_EPS = 1e-12

CONV_COLS = 3 * CONV_DIM
RWKV_COLS = 3 * RWKV_DIM + 2 * DECAY_LORA + 2 * AAA_LORA + GATE_LORA
IN_COLS = CONV_COLS + RWKV_COLS
RWKV_SPLITS = (RWKV_DIM, 2 * RWKV_DIM, 3 * RWKV_DIM,
               3 * RWKV_DIM + DECAY_LORA, 3 * RWKV_DIM + 2 * DECAY_LORA,
               3 * RWKV_DIM + 2 * DECAY_LORA + AAA_LORA,
               3 * RWKV_DIM + 2 * DECAY_LORA + 2 * AAA_LORA)

kernel_name = "hymba_conv_rwkv7_macaron_encoder"


def rms_norm(x, g):
    xf = x.astype(jnp.float32)
    y = xf * lax.rsqrt(jnp.mean(xf * xf, axis=-1, keepdims=True) + RMS_EPS)
    return (y * g.astype(jnp.float32)).astype(x.dtype)


def swiglu(h, w_gate, w_up, w_down):
    return (jax.nn.silu(h @ w_gate) * (h @ w_up)) @ w_down


def centred_token_shift(z, mu):
    zp = jnp.pad(z, ((0, 0), (1, 1), (0, 0)))
    nb = 0.5 * (zp[:, :-2] + zp[:, 2:])
    return z + mu * (nb - z)


def short_conv_mixer(zc, conv_w):
    b_gate, c_gate, xin = jnp.split(zc, 3, axis=-1)
    u = c_gate * xin
    up = jnp.pad(u, ((0, 0), (1, 1), (0, 0)))
    conv = conv_w[0] * up[:, :-2] + conv_w[1] * up[:, 1:-1] + conv_w[2] * up[:, 2:]
    return b_gate * conv


def wkv7_scan(r, decay, k, v, kk, a, reverse):
    bsz = r.shape[0]
    s0 = jnp.zeros((bsz, N_RWKV_HEADS, HEAD_SIZE, HEAD_SIZE), jnp.float32)

    def step(s, inp):
        r_t, w_t, k_t, v_t, kk_t, a_t = inp
        sa = jnp.einsum('bhvk,bhk->bhv', s, -kk_t)
        s = (s * w_t[:, :, None, :]
             + jnp.einsum('bhv,bhk->bhvk', sa, kk_t * a_t)
             + jnp.einsum('bhv,bhk->bhvk', v_t, k_t))
        o = jnp.einsum('bhvk,bhk->bhv', s, r_t)
        return s, o

    xs = tuple(jnp.moveaxis(t, 1, 0) for t in (r, decay, k, v, kk, a))
    _, o = lax.scan(step, s0, xs, reverse=reverse)
    return jnp.moveaxis(o, 0, 1)


def rwkv7_mixer(zr, w0, w2, a0, a2, g2, k_k, k_a, r_k, ln_w, ln_b):
    bsz, t_len, _ = zr.shape
    f32 = jnp.float32
    zr = zr.astype(f32)
    r, k, v, zw_f, zw_b, za_f, za_b, zg = jnp.split(zr, RWKV_SPLITS, axis=-1)
    heads = lambda t: t.reshape(bsz, t_len, N_RWKV_HEADS, HEAD_SIZE)

    kk = heads(k * k_k.astype(f32))
    kk = kk / jnp.maximum(jnp.linalg.norm(kk, axis=-1, keepdims=True), L2_EPS)
    g = jax.nn.sigmoid(zg) @ g2.astype(f32)

    o_sum = jnp.zeros((bsz, t_len, N_RWKV_HEADS, HEAD_SIZE), f32)
    bonus = jnp.zeros_like(o_sum)
    for d, (zw, za) in enumerate(((zw_f, za_f), (zw_b, za_b))):
        w_log = -jax.nn.softplus(-(w0[d].astype(f32) + jnp.tanh(zw) @ w2[d].astype(f32))) - 0.5
        decay = jnp.exp(-jnp.exp(w_log))
        a = jax.nn.sigmoid(a0[d].astype(f32) + za @ a2[d].astype(f32))
        k_d = k * (1.0 + (a - 1.0) * k_a.astype(f32))
        o_sum = o_sum + wkv7_scan(heads(r), heads(decay), heads(k_d), heads(v),
                                  kk, heads(a), reverse=(d == 1))
        bonus = bonus + jnp.sum(heads(r * k_d * r_k.astype(f32)), axis=-1, keepdims=True) * heads(v)

    mean = jnp.mean(o_sum, axis=-1, keepdims=True)
    var = jnp.mean(jnp.square(o_sum - mean), axis=-1, keepdims=True)
    o = (o_sum - mean) * lax.rsqrt(var + GN_EPS)
    o = o.reshape(bsz, t_len, RWKV_DIM) * ln_w.astype(f32) + ln_b.astype(f32)
    o = o + bonus.reshape(bsz, t_len, RWKV_DIM)
    return o * g


def encoder_layer(x, ffn1_norm, ffn1_w_gate, ffn1_w_up, ffn1_w_down,
                  mix_norm, w_in, conv_w, mu_shift, w0, w2, a0, a2, g2,
                  k_k, k_a, r_k, ln_x_w, ln_x_b, w_out,
                  ffn2_norm, ffn2_w_gate, ffn2_w_up, ffn2_w_down):
    x = x + FFN_RESIDUAL_SCALE * swiglu(rms_norm(x, ffn1_norm), ffn1_w_gate, ffn1_w_up, ffn1_w_down)
    h = rms_norm(x, mix_norm)
    z = h @ w_in
    zc, zr = z[..., :CONV_COLS], z[..., CONV_COLS:]
    y_conv = short_conv_mixer(zc, conv_w)
    y_rwkv = rwkv7_mixer(centred_token_shift(zr, mu_shift), w0, w2, a0, a2, g2,
                         k_k, k_a, r_k, ln_x_w, ln_x_b).astype(x.dtype)
    x = x + jnp.concatenate([y_conv, y_rwkv], axis=-1) @ w_out
    x = x + FFN_RESIDUAL_SCALE * swiglu(rms_norm(x, ffn2_norm), ffn2_w_gate, ffn2_w_up, ffn2_w_down)
    return x


def setup_inputs(seed: int = 0) -> dict:
    key = jax.random.key(seed)
    ks = jax.random.split(key, 32)
    f32 = jnp.float32
    nrm = lambda k, shape, scale: jax.random.normal(k, shape, f32) * scale
    gain = lambda k, shape: 1.0 + 0.02 * jax.random.normal(k, shape, f32)
    L = DEPTH
    return {
        "x_prompt": nrm(ks[0], (BATCH, SEQ, D_MODEL), 1.0),
        "x_sample": nrm(ks[1], (DEC_BATCH, DEC_SEQ, D_MODEL), 1.0),
        "ffn1_norm": gain(ks[2], (L, D_MODEL)),
        "ffn1_w_gate": nrm(ks[3], (L, D_MODEL, D_FF), D_MODEL ** -0.5),
        "ffn1_w_up": nrm(ks[4], (L, D_MODEL, D_FF), D_MODEL ** -0.5),
        "ffn1_w_down": nrm(ks[5], (L, D_FF, D_MODEL), D_FF ** -0.5),
        "mix_norm": gain(ks[6], (L, D_MODEL)),
        "w_in": nrm(ks[7], (L, D_MODEL, IN_COLS), D_MODEL ** -0.5),
        "conv_w": nrm(ks[8], (L, CONV_WIDTH, CONV_DIM), 0.5),
        "mu_shift": jax.random.uniform(ks[9], (L, RWKV_COLS), f32, 0.0, 0.5),
        "w0": jax.random.uniform(ks[10], (L, 2, RWKV_DIM), f32, -6.0, -1.0),
        "w2": nrm(ks[11], (L, 2, DECAY_LORA, RWKV_DIM), 0.1),
        "a0": nrm(ks[12], (L, 2, RWKV_DIM), 0.1),
        "a2": nrm(ks[13], (L, 2, AAA_LORA, RWKV_DIM), 0.1),
        "g2": nrm(ks[14], (L, GATE_LORA, RWKV_DIM), GATE_LORA ** -0.5),
        "k_k": 0.85 + nrm(ks[15], (L, RWKV_DIM), 0.02),
        "k_a": gain(ks[16], (L, RWKV_DIM)),
        "r_k": nrm(ks[17], (L, RWKV_DIM), 0.1),
        "ln_x_w": gain(ks[18], (L, RWKV_DIM)),
        "ln_x_b": nrm(ks[19], (L, RWKV_DIM), 0.01),
        "w_out": nrm(ks[20], (L, D_MODEL, D_MODEL), D_MODEL ** -0.5),
        "ffn2_norm": gain(ks[21], (L, D_MODEL)),
        "ffn2_w_gate": nrm(ks[22], (L, D_MODEL, D_FF), D_MODEL ** -0.5),
        "ffn2_w_up": nrm(ks[23], (L, D_MODEL, D_FF), D_MODEL ** -0.5),
        "ffn2_w_down": nrm(ks[24], (L, D_FF, D_MODEL), D_FF ** -0.5),
        "final_norm": gain(ks[25], (D_MODEL,)),
    }


def reference(x_prompt, x_sample, ffn1_norm, ffn1_w_gate, ffn1_w_up, ffn1_w_down,
              mix_norm, w_in, conv_w, mu_shift, w0, w2, a0, a2, g2,
              k_k, k_a, r_k, ln_x_w, ln_x_b, w_out,
              ffn2_norm, ffn2_w_gate, ffn2_w_up, ffn2_w_down, final_norm):
    def trunk(x):
        for l in range(DEPTH):
            x = encoder_layer(x, ffn1_norm[l], ffn1_w_gate[l], ffn1_w_up[l], ffn1_w_down[l],
                              mix_norm[l], w_in[l], conv_w[l], mu_shift[l], w0[l], w2[l],
                              a0[l], a2[l], g2[l], k_k[l], k_a[l], r_k[l],
                              ln_x_w[l], ln_x_b[l], w_out[l],
                              ffn2_norm[l], ffn2_w_gate[l], ffn2_w_up[l], ffn2_w_down[l])
        return rms_norm(x, final_norm)

    y_prompt = trunk(x_prompt)
    y_sample = trunk(x_sample)
    return (y_prompt, y_sample)
```

```python
import functools

import jax
import jax.numpy as jnp
from jax import lax
from jax.experimental import pallas as pl
from jax.experimental.pallas import tpu as pltpu

F32 = jnp.float32
BF16 = jnp.bfloat16
HIGHEST = lax.Precision.HIGHEST

HEAD_SIZE = 64
LANES = 128
SUBLANES = 8
CHUNK = 64
DECAY_LORA = 64
AAA_LORA = 64
GATE_LORA = 160
LORA_COLS = 2 * DECAY_LORA + 2 * AAA_LORA
GATE_PAD = 256
FFN_RESIDUAL_SCALE = 0.5
RMS_EPS = 1e-6
GN_EPS = 64e-5
L2_EPS = 1e-12
VMEM_LIMIT = 56 * 1024 * 1024


def _params(sem):
    return pltpu.CompilerParams(dimension_semantics=sem, vmem_limit_bytes=VMEM_LIMIT)


def _rms(x, w):
    return x * lax.rsqrt(jnp.mean(x * x, axis=-1, keepdims=True) + RMS_EPS) * w


def _sigmoid(x):
    return 1.0 / (1.0 + jnp.exp(-x))


def _mm(a, b):
    return jnp.dot(a.astype(BF16), b.astype(BF16), preferred_element_type=F32)


def _ffn_kernel(x_ref, nw_ref, wg_ref, wu_ref, wd_ref, fw_ref, o_ref, h_scr, acc_scr, *, final):
    j = pl.program_id(1)

    @pl.when(j == 0)
    def _():
        h_scr[...] = _rms(x_ref[...], nw_ref[...]).astype(BF16)
        acc_scr[...] = jnp.zeros_like(acc_scr)

    h = h_scr[...]
    g = jnp.dot(h, wg_ref[...], preferred_element_type=F32)
    u = jnp.dot(h, wu_ref[...], preferred_element_type=F32)
    a = (g * _sigmoid(g)) * u
    acc_scr[...] += jnp.dot(a.astype(BF16), wd_ref[...], preferred_element_type=F32)

    @pl.when(j == pl.num_programs(1) - 1)
    def _():
        y = x_ref[...] + FFN_RESIDUAL_SCALE * acc_scr[...]
        if final:
            y = _rms(y, fw_ref[...])
        o_ref[...] = y


def _ffn(x, norm_w, wg, wu, wd, final_w, *, final, tm, tf):
    n, d = x.shape
    dff = wg.shape[1]
    return pl.pallas_call(
        functools.partial(_ffn_kernel, final=final),
        grid=(n // tm, dff // tf),
        in_specs=[
            pl.BlockSpec((tm, d), lambda i, j: (i, 0)),
            pl.BlockSpec((1, d), lambda i, j: (0, 0)),
            pl.BlockSpec((d, tf), lambda i, j: (0, j)),
            pl.BlockSpec((d, tf), lambda i, j: (0, j)),
            pl.BlockSpec((tf, d), lambda i, j: (j, 0)),
            pl.BlockSpec((1, d), lambda i, j: (0, 0)),
        ],
        out_specs=pl.BlockSpec((tm, d), lambda i, j: (i, 0)),
        out_shape=jax.ShapeDtypeStruct((n, d), F32),
        scratch_shapes=[pltpu.VMEM((tm, d), BF16), pltpu.VMEM((tm, d), F32)],
        compiler_params=_params(("parallel", "arbitrary")),
        name="ffn_final" if final else "ffn",
    )(x, norm_w, wg, wu, wd, final_w)


def _inproj_kernel(x_ref, nw_ref, w_ref, o_ref, h_scr):
    @pl.when(pl.program_id(1) == 0)
    def _():
        h_scr[...] = _rms(x_ref[...], nw_ref[...]).astype(BF16)

    o_ref[...] = jnp.dot(h_scr[...], w_ref[...], preferred_element_type=F32)


def _inproj(x, norm_w, w, *, tm, tn):
    n, d = x.shape
    cols = w.shape[1]
    return pl.pallas_call(
        _inproj_kernel,
        grid=(n // tm, cols // tn),
        in_specs=[
            pl.BlockSpec((tm, d), lambda i, j: (i, 0)),
            pl.BlockSpec((1, d), lambda i, j: (0, 0)),
            pl.BlockSpec((d, tn), lambda i, j: (0, j)),
        ],
        out_specs=pl.BlockSpec((tm, tn), lambda i, j: (i, j)),
        out_shape=jax.ShapeDtypeStruct((n, cols), F32),
        scratch_shapes=[pltpu.VMEM((tm, d), BF16)],
        compiler_params=_params(("parallel", "arbitrary")),
        name="inproj",
    )(x, norm_w, w)


def _head_ones():
    r = lax.broadcasted_iota(jnp.int32, (LANES, LANES), 0) // HEAD_SIZE
    c = lax.broadcasted_iota(jnp.int32, (LANES, LANES), 1) // HEAD_SIZE
    return (r == c).astype(F32)


def _head_sum(x, ones_bd):
    groups = x.shape[1] // LANES
    parts = [jnp.dot(x[:, i * LANES:(i + 1) * LANES], ones_bd, precision=HIGHEST,
                     preferred_element_type=F32) for i in range(groups)]
    return parts[0] if groups == 1 else jnp.concatenate(parts, axis=1)


def _shift_rows(x, prev_row, next_row):
    rows = x.shape[0]
    ridx = lax.broadcasted_iota(jnp.int32, x.shape, 0)
    dn = jnp.where(ridx == 0, prev_row, pltpu.roll(x, 1, axis=0))
    up = jnp.where(ridx == rows - 1, next_row, pltpu.roll(x, rows - 1, axis=0))
    return dn, up


def _prep_kernel(z_ref, zp_ref, zn_ref, mu_ref, cw_ref, w0_ref, a0_ref, kk_ref, ka_ref, rk_ref,
                 w2f_ref, w2b_ref, a2f_ref, a2b_ref, g2_ref,
                 yc_ref, r_ref, v_ref, kkn_ref, kkaf_ref, kkab_ref, kf_ref, kb_ref,
                 ldf_ref, ldb_ref, bonus_ref, g_ref, *, tiles_per_seq, cdim, rdim):
    i = pl.program_id(0)
    has_prev = (i % tiles_per_seq != 0).astype(F32)
    has_next = (i % tiles_per_seq != tiles_per_seq - 1).astype(F32)
    prow = SUBLANES - 1

    b_gate = z_ref[:, 0:cdim]
    u = z_ref[:, cdim:2 * cdim] * z_ref[:, 2 * cdim:3 * cdim]
    u_prev = zp_ref[prow:prow + 1, cdim:2 * cdim] * zp_ref[prow:prow + 1, 2 * cdim:3 * cdim] * has_prev
    u_next = zn_ref[0:1, cdim:2 * cdim] * zn_ref[0:1, 2 * cdim:3 * cdim] * has_next
    u_dn, u_up = _shift_rows(u, u_prev, u_next)
    yc_ref[...] = b_gate * (cw_ref[0:1, :] * u_dn + cw_ref[1:2, :] * u + cw_ref[2:3, :] * u_up)

    c0 = 3 * cdim

    def shifted(lo, hi):
        zc = z_ref[:, c0 + lo:c0 + hi]
        dn, up = _shift_rows(zc, zp_ref[prow:prow + 1, c0 + lo:c0 + hi] * has_prev,
                             zn_ref[0:1, c0 + lo:c0 + hi] * has_next)
        return zc + mu_ref[:, lo:hi] * (0.5 * (dn + up) - zc)

    r = shifted(0, rdim)
    k = shifted(rdim, 2 * rdim)
    v = shifted(2 * rdim, 3 * rdim)
    zl = shifted(3 * rdim, 3 * rdim + LORA_COLS)
    zg = shifted(3 * rdim + LORA_COLS, 3 * rdim + LORA_COLS + GATE_PAD)

    ones_bd = _head_ones()
    kk = k * kk_ref[...]
    nrm = jnp.sqrt(_head_sum(kk * kk, ones_bd))
    kkn = kk / jnp.maximum(nrm, L2_EPS)
    g_ref[...] = _mm(_sigmoid(zg), g2_ref[...])
    r_ref[...] = r
    v_ref[...] = v
    kkn_ref[...] = kkn

    th = jnp.tanh(zl[:, 0:2 * DECAY_LORA])
    za = zl[:, 2 * DECAY_LORA:LORA_COLS]
    rk_sum = jnp.zeros_like(r)
    for d, (w2_ref, a2_ref, kka_ref, kd_ref, ld_ref) in enumerate(
            ((w2f_ref, a2f_ref, kkaf_ref, kf_ref, ldf_ref), (w2b_ref, a2b_ref, kkab_ref, kb_ref, ldb_ref))):
        y = -(w0_ref[d:d + 1, :] + _mm(th, w2_ref[...]))
        softplus = jnp.maximum(y, 0.0) + jnp.log1p(jnp.exp(-jnp.abs(y)))
        ld_ref[...] = -jnp.exp(-softplus - 0.5)
        a = _sigmoid(a0_ref[d:d + 1, :] + _mm(za, a2_ref[...]))
        k_d = k * (1.0 + (a - 1.0) * ka_ref[...])
        kka_ref[...] = kkn * a
        kd_ref[...] = k_d
        rk_sum = rk_sum + r * k_d * rk_ref[...]
    bonus_ref[...] = _head_sum(rk_sum, ones_bd) * v


def _prep(z, mu, conv_w, w0, a0, k_k, k_a, r_k, w2f, w2b, a2f, a2b, g2p, *, seq_len, tm, cdim, rdim):
    n, cols = z.shape
    sub_per_tile = tm // SUBLANES
    last_sub = n // SUBLANES - 1
    full = lambda a: pl.BlockSpec(a.shape, lambda i: (0,) * a.ndim)
    out_spec = pl.BlockSpec((tm, rdim), lambda i: (i, 0))
    consts = (mu, conv_w, w0, a0, k_k, k_a, r_k, w2f, w2b, a2f, a2b, g2p)
    return pl.pallas_call(
        functools.partial(_prep_kernel, tiles_per_seq=seq_len // tm, cdim=cdim, rdim=rdim),
        grid=(n // tm,),
        in_specs=[
            pl.BlockSpec((tm, cols), lambda i: (i, 0)),
            pl.BlockSpec((SUBLANES, cols), lambda i: (jnp.maximum(i * sub_per_tile - 1, 0), 0)),
            pl.BlockSpec((SUBLANES, cols), lambda i: (jnp.minimum((i + 1) * sub_per_tile, last_sub), 0)),
        ] + [full(a) for a in consts],
        out_specs=[pl.BlockSpec((tm, cdim), lambda i: (i, 0))] + [out_spec] * 11,
        out_shape=[jax.ShapeDtypeStruct((n, cdim), F32)] + [jax.ShapeDtypeStruct((n, rdim), F32)] * 11,
        compiler_params=_params(("parallel",)),
        name="prep",
    )(z, z, z, *consts)


def _block_diag(y, head0):
    return jnp.concatenate([jnp.where(head0, y, 0.0), jnp.where(head0, 0.0, y)], axis=0)


def _scan_chunk_pair(at, bt, kt, rt, bh, kh, v, decay_c, h_bd, masks):
    head0, strict, incl, eye, bd_mask, diag_mask = masks
    c = CHUNK
    lhs = jnp.concatenate([at, rt], axis=0)
    rhs = jnp.concatenate([_block_diag(bt, head0), _block_diag(kt, head0)], axis=0)
    a_all = lax.dot_general(lhs.astype(BF16), rhs.astype(BF16), (((1,), (1,)), ((), ())),
                            preferred_element_type=F32)
    a_ab = jnp.where(strict, a_all[:c, :LANES], 0.0)
    a_ak = jnp.where(strict, a_all[:c, LANES:], 0.0)
    a_rb = jnp.where(incl, a_all[c:, :LANES], 0.0)
    a_rk = jnp.where(incl, a_all[c:, LANES:], 0.0)

    t_inv = eye + a_ab
    a_pow = a_ab
    steps = CHUNK.bit_length() - 2
    for _ in range(steps):
        a_pow = _mm(a_pow, _block_diag(a_pow, head0))
        t_inv = t_inv + _mm(t_inv, _block_diag(a_pow, head0))

    v_bd = _block_diag(v, head0)
    x = _mm(a_ak, v_bd)
    ap_w = _mm(t_inv, jnp.concatenate([_block_diag(at, head0), _block_diag(x, head0)], axis=1))
    ap = ap_w[:, :LANES]
    w = ap_w[:, LANES:]
    q_o = _mm(a_rb, jnp.concatenate([_block_diag(ap, head0), _block_diag(w, head0)], axis=1))
    qp = rt + q_o[:, :LANES]
    o_intra = q_o[:, LANES:] + _mm(a_rk, v_bd)

    lt = jnp.transpose(jnp.concatenate([bh, kh], axis=0))
    rhs2 = jnp.concatenate([jnp.concatenate([ap, w], axis=1),
                            jnp.concatenate([jnp.zeros_like(v), v], axis=1)], axis=0)
    mn = _mm(lt, rhs2)
    m_bd = jnp.where(bd_mask, mn[:, :LANES], 0.0) + jnp.where(diag_mask, decay_c, 0.0)
    n_bd = jnp.where(bd_mask, mn[:, LANES:], 0.0)

    o = _mm(qp, h_bd) + o_intra
    h_new = _mm(m_bd, h_bd) + n_bd
    return o, h_new


def _scan_kernel(r_ref, v_ref, kk_ref, kka_ref, kd_ref, ld_ref, o_ref, h_scr, *, reverse, pairs, chunks):
    c = CHUNK

    @pl.when(pl.program_id(1) == 0)
    def _():
        h_scr[...] = jnp.zeros_like(h_scr)

    t_i = lax.broadcasted_iota(jnp.int32, (c, c), 0)
    s_i = lax.broadcasted_iota(jnp.int32, (c, c), 1)
    tri = ((s_i >= t_i) if reverse else (s_i <= t_i)).astype(F32)
    tp = lax.broadcasted_iota(jnp.int32, (c, LANES), 0)
    lane = lax.broadcasted_iota(jnp.int32, (c, LANES), 1)
    sp = lane % HEAD_SIZE
    head0 = lane < HEAD_SIZE
    strict = (sp > tp) if reverse else (sp < tp)
    incl = (sp >= tp) if reverse else (sp <= tp)
    eye = (sp == tp).astype(F32)
    r128 = lax.broadcasted_iota(jnp.int32, (LANES, LANES), 0)
    c128 = lax.broadcasted_iota(jnp.int32, (LANES, LANES), 1)
    bd_mask = (r128 < HEAD_SIZE) == (c128 < HEAD_SIZE)
    diag_mask = r128 == c128
    masks = (head0, strict, incl, eye, bd_mask, diag_mask)
    last = 0 if reverse else c - 1

    def chunk_body(ci, carry):
        cc = (chunks - 1 - ci) if reverse else ci
        rows = pl.ds(pl.multiple_of(cc * c, c), c)
        ld = ld_ref[rows, :]
        cum = jnp.dot(tri, ld, precision=HIGHEST, preferred_element_type=F32)
        cum_c = cum[last:last + 1, :]
        e_neg = jnp.exp(-cum)
        e_tail = jnp.exp(cum_c - cum)
        kk = kk_ref[rows, :]
        kka = kka_ref[rows, :]
        kd = kd_ref[rows, :]
        at = -kk * jnp.exp(cum - ld)
        rt = r_ref[rows, :] * jnp.exp(cum)
        bt = kka * e_neg
        kt = kd * e_neg
        bh = kka * e_tail
        kh = kd * e_tail
        decay_c = jnp.exp(cum_c)
        v = v_ref[rows, :]
        for p in range(pairs):
            sl = slice(p * LANES, (p + 1) * LANES)
            o, h_new = _scan_chunk_pair(at[:, sl], bt[:, sl], kt[:, sl], rt[:, sl], bh[:, sl], kh[:, sl],
                                        v[:, sl], decay_c[:, sl], h_scr[p], masks)
            o_ref[rows, sl] = o
            h_scr[p] = h_new
        return carry

    lax.fori_loop(0, chunks, chunk_body, 0)


def _scan(r, v, kk, kka, kd, ld, *, reverse, seq_len, tb):
    n, rdim = r.shape
    n_seq = n // seq_len
    bps = seq_len // tb
    pairs = rdim // LANES

    def idx(s, j):
        jj = (bps - 1 - j) if reverse else j
        return (s * bps + jj, 0)

    spec = pl.BlockSpec((tb, rdim), idx)
    return pl.pallas_call(
        functools.partial(_scan_kernel, reverse=reverse, pairs=pairs, chunks=tb // CHUNK),
        grid=(n_seq, bps),
        in_specs=[spec] * 6,
        out_specs=spec,
        out_shape=jax.ShapeDtypeStruct((n, rdim), F32),
        scratch_shapes=[pltpu.VMEM((pairs, LANES, LANES), F32)],
        compiler_params=_params(("parallel", "arbitrary")),
        name="scan_bwd" if reverse else "scan_fwd",
    )(r, v, kk, kka, kd, ld)


def _post_kernel(of_ref, ob_ref, bonus_ref, g_ref, yc_ref, x_ref, lnw_ref, lnb_ref, wc_ref, wr_ref, o_ref):
    ones_bd = _head_ones()
    o = of_ref[...] + ob_ref[...]
    mean = _head_sum(o, ones_bd) * (1.0 / HEAD_SIZE)
    d = o - mean
    var = _head_sum(d * d, ones_bd) * (1.0 / HEAD_SIZE)
    y = d * lax.rsqrt(var + GN_EPS) * lnw_ref[...] + lnb_ref[...] + bonus_ref[...]
    y = y * g_ref[...]
    o_ref[...] = x_ref[...] + _mm(yc_ref[...], wc_ref[...]) + _mm(y, wr_ref[...])


def _post(o_f, o_b, bonus, g, yc, x, ln_w, ln_b, w_conv, w_rwkv, *, tm):
    n, d = x.shape
    cdim = yc.shape[1]
    rdim = o_f.shape[1]
    rspec = pl.BlockSpec((tm, rdim), lambda i: (i, 0))
    full = lambda a: pl.BlockSpec(a.shape, lambda i: (0,) * a.ndim)
    return pl.pallas_call(
        _post_kernel,
        grid=(n // tm,),
        in_specs=[rspec, rspec, rspec, rspec, pl.BlockSpec((tm, cdim), lambda i: (i, 0)),
                  pl.BlockSpec((tm, d), lambda i: (i, 0)), full(ln_w), full(ln_b), full(w_conv), full(w_rwkv)],
        out_specs=pl.BlockSpec((tm, d), lambda i: (i, 0)),
        out_shape=jax.ShapeDtypeStruct((n, d), F32),
        compiler_params=_params(("parallel",)),
        name="post",
    )(o_f, o_b, bonus, g, yc, x, ln_w, ln_b, w_conv, w_rwkv)


def _tile(n, pref):
    t = min(pref, n)
    assert n % t == 0, (n, t)
    return t


def _pad_rows(w, rows_before, rows_total):
    return jnp.pad(w, ((rows_before, rows_total - rows_before - w.shape[0]), (0, 0)))


def _layer(x, seq_len, p):
    n, d = x.shape
    cdim = p["conv_w"].shape[1]
    rdim = p["k_k"].shape[1]
    tm = _tile(seq_len, 512)
    dff = p["ffn1_wg"].shape[1]
    tf = _tile(dff, 512)
    x1 = _ffn(x, p["ffn1_norm"], p["ffn1_wg"], p["ffn1_wu"], p["ffn1_wd"], p["final_norm"],
              final=False, tm=tm, tf=tf)
    in_cols = p["w_in"].shape[1]
    tn = in_cols // 4 if in_cols % (4 * LANES) == 0 else in_cols
    z = _inproj(x1, p["mix_norm"], p["w_in"], tm=tm, tn=tn)
    tp = _tile(seq_len, 256)
    (yc, r, v, kk, kka_f, kka_b, k_f, k_b, ld_f, ld_b, bonus, g) = _prep(
        z, p["mu"], p["conv_w"], p["w0"], p["a0"], p["k_k"], p["k_a"], p["r_k"],
        p["w2f"], p["w2b"], p["a2f"], p["a2b"], p["g2"], seq_len=seq_len, tm=tp, cdim=cdim, rdim=rdim)
    tb = _tile(seq_len, 256)
    o_f = _scan(r, v, kk, kka_f, k_f, ld_f, reverse=False, seq_len=seq_len, tb=tb)
    o_b = _scan(r, v, kk, kka_b, k_b, ld_b, reverse=True, seq_len=seq_len, tb=tb)
    x2 = _post(o_f, o_b, bonus, g, yc, x1, p["ln_w"], p["ln_b"], p["w_out_c"], p["w_out_r"], tm=tp)
    return _ffn(x2, p["ffn2_norm"], p["ffn2_wg"], p["ffn2_wu"], p["ffn2_wd"], p["final_norm"],
                final=True, tm=tm, tf=tf)


def kernel(x_prompt, x_sample, ffn1_norm, ffn1_w_gate, ffn1_w_up, ffn1_w_down, mix_norm, w_in, conv_w, mu_shift,
           w0, w2, a0, a2, g2, k_k, k_a, r_k, ln_x_w, ln_x_b, w_out, ffn2_norm, ffn2_w_gate, ffn2_w_up,
           ffn2_w_down, final_norm):
    assert ffn1_norm.shape[0] == 1, "single-layer trunk"
    d = x_prompt.shape[-1]
    cdim = conv_w.shape[-1]
    rdim = k_k.shape[-1]
    in_cols = w_in.shape[-1]
    rwkv_cols = in_cols - 3 * cdim
    assert rwkv_cols == 3 * rdim + LORA_COLS + GATE_LORA
    pad_cols = 3 * rdim + LORA_COLS + GATE_PAD - rwkv_cols
    row = lambda a: a.reshape(1, -1).astype(F32)
    bf = lambda a: a.astype(BF16)
    p = {
        "ffn1_norm": row(ffn1_norm[0]), "ffn1_wg": bf(ffn1_w_gate[0]), "ffn1_wu": bf(ffn1_w_up[0]),
        "ffn1_wd": bf(ffn1_w_down[0]),
        "ffn2_norm": row(ffn2_norm[0]), "ffn2_wg": bf(ffn2_w_gate[0]), "ffn2_wu": bf(ffn2_w_up[0]),
        "ffn2_wd": bf(ffn2_w_down[0]),
        "final_norm": row(final_norm), "mix_norm": row(mix_norm[0]),
        "w_in": bf(jnp.pad(w_in[0], ((0, 0), (0, pad_cols)))),
        "mu": jnp.pad(mu_shift[0], (0, pad_cols)).reshape(1, -1),
        "conv_w": conv_w[0], "w0": w0[0], "a0": a0[0],
        "k_k": row(k_k[0]), "k_a": row(k_a[0]), "r_k": row(r_k[0]),
        "w2f": bf(_pad_rows(w2[0, 0], 0, 2 * DECAY_LORA)), "w2b": bf(_pad_rows(w2[0, 1], DECAY_LORA, 2 * DECAY_LORA)),
        "a2f": bf(_pad_rows(a2[0, 0], 0, 2 * AAA_LORA)), "a2b": bf(_pad_rows(a2[0, 1], AAA_LORA, 2 * AAA_LORA)),
        "g2": bf(_pad_rows(g2[0], 0, GATE_PAD)),
        "ln_w": row(ln_x_w[0]), "ln_b": row(ln_x_b[0]),
        "w_out_c": bf(w_out[0, :cdim]), "w_out_r": bf(w_out[0, cdim:]),
    }
    outs = []
    for x in (x_prompt, x_sample):
        b, t, _ = x.shape
        outs.append(_layer(x.reshape(b * t, d), t, p).reshape(b, t, d))
    return tuple(outs)
```

```python
import functools

import jax
import jax.numpy as jnp
from jax import lax
from jax.experimental import pallas as pl
from jax.experimental.pallas import tpu as pltpu

F32 = jnp.float32
BF16 = jnp.bfloat16
HIGHEST = lax.Precision.HIGHEST

HEAD_SIZE = 64
LANES = 128
SUBLANES = 8
CHUNK = 64
DECAY_LORA = 64
AAA_LORA = 64
GATE_LORA = 160
LORA_COLS = 2 * DECAY_LORA + 2 * AAA_LORA
GATE_PAD = 256
FFN_RESIDUAL_SCALE = 0.5
RMS_EPS = 1e-6
GN_EPS = 64e-5
L2_EPS = 1e-12
DECAY_SCALE = 0.6065306597126334
VMEM_LIMIT = 56 * 1024 * 1024


def _params(sem):
    return pltpu.CompilerParams(dimension_semantics=sem, vmem_limit_bytes=VMEM_LIMIT)


def _rms(x, w):
    return x * lax.rsqrt(jnp.mean(x * x, axis=-1, keepdims=True) + RMS_EPS) * w


def _sigmoid(x):
    return 1.0 / (1.0 + jnp.exp(-x))


def _mm(a, b):
    return jnp.dot(a.astype(BF16), b.astype(BF16), preferred_element_type=F32)


def _ffn_kernel(x_ref, nw_ref, wg_ref, wu_ref, wd_ref, fw_ref, o_ref, h_scr, acc_scr, *, final):
    j = pl.program_id(1)

    @pl.when(j == 0)
    def _():
        h_scr[...] = _rms(x_ref[...], nw_ref[...]).astype(BF16)
        acc_scr[...] = jnp.zeros_like(acc_scr)

    h = h_scr[...]
    g = jnp.dot(h, wg_ref[...], preferred_element_type=F32)
    u = jnp.dot(h, wu_ref[...], preferred_element_type=F32)
    a = (g * _sigmoid(g)) * u
    acc_scr[...] += jnp.dot(a.astype(BF16), wd_ref[...], preferred_element_type=F32)

    @pl.when(j == pl.num_programs(1) - 1)
    def _():
        y = x_ref[...] + FFN_RESIDUAL_SCALE * acc_scr[...]
        if final:
            y = _rms(y, fw_ref[...])
        o_ref[...] = y


def _ffn(x, norm_w, wg, wu, wd, final_w, *, final, tm, tf):
    n, d = x.shape
    dff = wg.shape[1]
    return pl.pallas_call(
        functools.partial(_ffn_kernel, final=final),
        grid=(n // tm, dff // tf),
        in_specs=[
            pl.BlockSpec((tm, d), lambda i, j: (i, 0)),
            pl.BlockSpec((1, d), lambda i, j: (0, 0)),
            pl.BlockSpec((d, tf), lambda i, j: (0, j)),
            pl.BlockSpec((d, tf), lambda i, j: (0, j)),
            pl.BlockSpec((tf, d), lambda i, j: (j, 0)),
            pl.BlockSpec((1, d), lambda i, j: (0, 0)),
        ],
        out_specs=pl.BlockSpec((tm, d), lambda i, j: (i, 0)),
        out_shape=jax.ShapeDtypeStruct((n, d), F32),
        scratch_shapes=[pltpu.VMEM((tm, d), BF16), pltpu.VMEM((tm, d), F32)],
        compiler_params=_params(("parallel", "arbitrary")),
        name="ffn_final" if final else "ffn",
    )(x, norm_w, wg, wu, wd, final_w)


def _inproj_kernel(x_ref, nw_ref, w_ref, o_ref, h_scr):
    @pl.when(pl.program_id(1) == 0)
    def _():
        h_scr[...] = _rms(x_ref[...], nw_ref[...]).astype(BF16)

    o_ref[...] = jnp.dot(h_scr[...], w_ref[...], preferred_element_type=F32)


def _inproj(x, norm_w, w, *, tm, tn):
    n, d = x.shape
    cols = w.shape[1]
    return pl.pallas_call(
        _inproj_kernel,
        grid=(n // tm, cols // tn),
        in_specs=[
            pl.BlockSpec((tm, d), lambda i, j: (i, 0)),
            pl.BlockSpec((1, d), lambda i, j: (0, 0)),
            pl.BlockSpec((d, tn), lambda i, j: (0, j)),
        ],
        out_specs=pl.BlockSpec((tm, tn), lambda i, j: (i, j)),
        out_shape=jax.ShapeDtypeStruct((n, cols), F32),
        scratch_shapes=[pltpu.VMEM((tm, d), BF16)],
        compiler_params=_params(("parallel", "arbitrary")),
        name="inproj",
    )(x, norm_w, w)


def _head_ones():
    r = lax.broadcasted_iota(jnp.int32, (LANES, LANES), 0) // HEAD_SIZE
    c = lax.broadcasted_iota(jnp.int32, (LANES, LANES), 1) // HEAD_SIZE
    return (r == c).astype(F32)


def _head_sum(x, ones_bd):
    groups = x.shape[1] // LANES
    parts = [jnp.dot(x[:, i * LANES:(i + 1) * LANES], ones_bd, precision=HIGHEST,
                     preferred_element_type=F32) for i in range(groups)]
    return parts[0] if groups == 1 else jnp.concatenate(parts, axis=1)


def _shift_rows(x, prev_row, next_row):
    rows = x.shape[0]
    ridx = lax.broadcasted_iota(jnp.int32, x.shape, 0)
    dn = jnp.where(ridx == 0, prev_row, pltpu.roll(x, 1, axis=0))
    up = jnp.where(ridx == rows - 1, next_row, pltpu.roll(x, rows - 1, axis=0))
    return dn, up


def _prep_kernel(z_ref, zp_ref, zn_ref, mu_ref, cw_ref, w0_ref, a0_ref, kk_ref, ka_ref, rk_ref,
                 w2f_ref, w2b_ref, a2f_ref, a2b_ref, g2_ref,
                 yc_ref, r_ref, v_ref, kkn_ref, kkaf_ref, kkab_ref, kf_ref, kb_ref,
                 ldf_ref, ldb_ref, bonus_ref, g_ref, *, tiles_per_seq, cdim, rdim):
    i = pl.program_id(0)
    has_prev = (i % tiles_per_seq != 0).astype(F32)
    has_next = (i % tiles_per_seq != tiles_per_seq - 1).astype(F32)
    prow = SUBLANES - 1

    b_gate = z_ref[:, 0:cdim]
    u = z_ref[:, cdim:2 * cdim] * z_ref[:, 2 * cdim:3 * cdim]
    u_prev = zp_ref[prow:prow + 1, cdim:2 * cdim] * zp_ref[prow:prow + 1, 2 * cdim:3 * cdim] * has_prev
    u_next = zn_ref[0:1, cdim:2 * cdim] * zn_ref[0:1, 2 * cdim:3 * cdim] * has_next
    u_dn, u_up = _shift_rows(u, u_prev, u_next)
    yc_ref[...] = b_gate * (cw_ref[0:1, :] * u_dn + cw_ref[1:2, :] * u + cw_ref[2:3, :] * u_up)

    c0 = 3 * cdim

    def shifted(lo, hi):
        zc = z_ref[:, c0 + lo:c0 + hi]
        dn, up = _shift_rows(zc, zp_ref[prow:prow + 1, c0 + lo:c0 + hi] * has_prev,
                             zn_ref[0:1, c0 + lo:c0 + hi] * has_next)
        return zc + mu_ref[:, lo:hi] * (0.5 * (dn + up) - zc)

    r = shifted(0, rdim)
    k = shifted(rdim, 2 * rdim)
    v = shifted(2 * rdim, 3 * rdim)
    zl = shifted(3 * rdim, 3 * rdim + LORA_COLS)
    zg = shifted(3 * rdim + LORA_COLS, 3 * rdim + LORA_COLS + GATE_PAD)

    ones_bd = _head_ones()
    kk = k * kk_ref[...]
    nrm = jnp.sqrt(_head_sum(kk * kk, ones_bd))
    kkn = kk / jnp.maximum(nrm, L2_EPS)
    g_ref[...] = _mm(_sigmoid(zg), g2_ref[...])
    r_ref[...] = r
    v_ref[...] = v
    kkn_ref[...] = kkn

    th = jnp.tanh(zl[:, 0:2 * DECAY_LORA])
    za = zl[:, 2 * DECAY_LORA:LORA_COLS]
    rk_sum = jnp.zeros_like(r)
    for d, (w2_ref, a2_ref, kka_ref, kd_ref, ld_ref) in enumerate(
            ((w2f_ref, a2f_ref, kkaf_ref, kf_ref, ldf_ref), (w2b_ref, a2b_ref, kkab_ref, kb_ref, ldb_ref))):
        y = w0_ref[d:d + 1, :] + _mm(th, w2_ref[...])
        ld_ref[...] = -DECAY_SCALE * _sigmoid(y)
        a = _sigmoid(a0_ref[d:d + 1, :] + _mm(za, a2_ref[...]))
        k_d = k * (1.0 + (a - 1.0) * ka_ref[...])
        kka_ref[...] = kkn * a
        kd_ref[...] = k_d
        rk_sum = rk_sum + r * k_d * rk_ref[...]
    bonus_ref[...] = _head_sum(rk_sum, ones_bd) * v


def _prep(z, mu, conv_w, w0, a0, k_k, k_a, r_k, w2f, w2b, a2f, a2b, g2p, *, seq_len, tm, cdim, rdim):
    n, cols = z.shape
    sub_per_tile = tm // SUBLANES
    last_sub = n // SUBLANES - 1
    full = lambda a: pl.BlockSpec(a.shape, lambda i: (0,) * a.ndim)
    out_spec = pl.BlockSpec((tm, rdim), lambda i: (i, 0))
    consts = (mu, conv_w, w0, a0, k_k, k_a, r_k, w2f, w2b, a2f, a2b, g2p)
    return pl.pallas_call(
        functools.partial(_prep_kernel, tiles_per_seq=seq_len // tm, cdim=cdim, rdim=rdim),
        grid=(n // tm,),
        in_specs=[
            pl.BlockSpec((tm, cols), lambda i: (i, 0)),
            pl.BlockSpec((SUBLANES, cols), lambda i: (jnp.maximum(i * sub_per_tile - 1, 0), 0)),
            pl.BlockSpec((SUBLANES, cols), lambda i: (jnp.minimum((i + 1) * sub_per_tile, last_sub), 0)),
        ] + [full(a) for a in consts],
        out_specs=[pl.BlockSpec((tm, cdim), lambda i: (i, 0))] + [out_spec] * 11,
        out_shape=[jax.ShapeDtypeStruct((n, cdim), F32)] + [jax.ShapeDtypeStruct((n, rdim), F32)] * 11,
        compiler_params=_params(("parallel",)),
        name="prep",
    )(z, z, z, *consts)


def _block_diag(y, head0):
    return jnp.concatenate([jnp.where(head0, y, 0.0), jnp.where(head0, 0.0, y)], axis=0)


def _scan_chunk(at, bt, kt, rt, bh, kh, v, decay_c, h_list, masks, pairs):
    head0, strict, incl, eye, bd_mask, diag_mask = masks
    c = CHUNK
    rng = range(pairs)
    sl = [slice(p * LANES, (p + 1) * LANES) for p in rng]
    bd = lambda y: _block_diag(y, head0)

    a_all = [lax.dot_general(
        jnp.concatenate([at[:, sl[p]], rt[:, sl[p]]], axis=0).astype(BF16),
        jnp.concatenate([bd(bt[:, sl[p]]), bd(kt[:, sl[p]])], axis=0).astype(BF16),
        (((1,), (1,)), ((), ())), preferred_element_type=F32) for p in rng]
    a_ab = [jnp.where(strict, a_all[p][:c, :LANES], 0.0) for p in rng]
    a_ak = [jnp.where(strict, a_all[p][:c, LANES:], 0.0) for p in rng]
    a_rb = [jnp.where(incl, a_all[p][c:, :LANES], 0.0) for p in rng]
    a_rk = [jnp.where(incl, a_all[p][c:, LANES:], 0.0) for p in rng]

    t_inv = [eye + a_ab[p] for p in rng]
    a_pow = [_mm(a_ab[p], bd(a_ab[p])) for p in rng]
    for _ in range(CHUNK.bit_length() - 3):
        both = [_mm(jnp.concatenate([t_inv[p], a_pow[p]], axis=0), bd(a_pow[p])) for p in rng]
        t_inv = [t_inv[p] + both[p][:c] for p in rng]
        a_pow = [both[p][c:] for p in rng]
    t_inv = [t_inv[p] + _mm(t_inv[p], bd(a_pow[p])) for p in rng]

    x = [_mm(a_ak[p], bd(v[:, sl[p]])) for p in rng]
    ap_w = [_mm(t_inv[p], jnp.concatenate([bd(at[:, sl[p]]), bd(x[p])], axis=1)) for p in rng]
    q_o = [_mm(a_rb[p], jnp.concatenate([bd(ap_w[p][:, :LANES]), bd(ap_w[p][:, LANES:])], axis=1)) for p in rng]
    o_rk = [_mm(a_rk[p], bd(v[:, sl[p]])) for p in rng]
    lt = [jnp.transpose(jnp.concatenate([bh[:, sl[p]], kh[:, sl[p]]], axis=0)) for p in rng]
    zero = jnp.zeros((c, LANES), F32)
    mn = [_mm(lt[p], jnp.concatenate([ap_w[p], jnp.concatenate([zero, v[:, sl[p]]], axis=1)], axis=0))
          for p in rng]

    outs, h_new = [], []
    for p in rng:
        qp = rt[:, sl[p]] + q_o[p][:, :LANES]
        m_bd = jnp.where(bd_mask, mn[p][:, :LANES], 0.0) + jnp.where(diag_mask, decay_c[:, sl[p]], 0.0)
        n_bd = jnp.where(bd_mask, mn[p][:, LANES:], 0.0)
        outs.append(_mm(qp, h_list[p]) + q_o[p][:, LANES:] + o_rk[p])
        h_new.append(_mm(m_bd, h_list[p]) + n_bd)
    return outs, h_new


def _scan_kernel(r_ref, v_ref, kk_ref, kka_ref, kd_ref, ld_ref, o_ref, h_scr, *, reverse, pairs, chunks):
    c = CHUNK

    @pl.when(pl.program_id(1) == 0)
    def _():
        h_scr[...] = jnp.zeros_like(h_scr)

    t_i = lax.broadcasted_iota(jnp.int32, (c, c), 0)
    s_i = lax.broadcasted_iota(jnp.int32, (c, c), 1)
    tri = ((s_i >= t_i) if reverse else (s_i <= t_i)).astype(F32)
    tp = lax.broadcasted_iota(jnp.int32, (c, LANES), 0)
    lane = lax.broadcasted_iota(jnp.int32, (c, LANES), 1)
    sp = lane % HEAD_SIZE
    head0 = lane < HEAD_SIZE
    strict = (sp > tp) if reverse else (sp < tp)
    incl = (sp >= tp) if reverse else (sp <= tp)
    eye = (sp == tp).astype(F32)
    r128 = lax.broadcasted_iota(jnp.int32, (LANES, LANES), 0)
    c128 = lax.broadcasted_iota(jnp.int32, (LANES, LANES), 1)
    bd_mask = (r128 < HEAD_SIZE) == (c128 < HEAD_SIZE)
    diag_mask = r128 == c128
    masks = (head0, strict, incl, eye, bd_mask, diag_mask)
    last = 0 if reverse else c - 1

    def chunk_body(ci, carry):
        cc = (chunks - 1 - ci) if reverse else ci
        rows = pl.ds(pl.multiple_of(cc * c, c), c)
        ld = ld_ref[rows, :]
        cum = jnp.dot(tri, ld, precision=HIGHEST, preferred_element_type=F32)
        cum_c = cum[last:last + 1, :]
        e_neg = jnp.exp(-cum)
        e_tail = jnp.exp(cum_c - cum)
        kk = kk_ref[rows, :]
        kka = kka_ref[rows, :]
        kd = kd_ref[rows, :]
        at = -kk * jnp.exp(cum - ld)
        rt = r_ref[rows, :] * jnp.exp(cum)
        outs, h_new = _scan_chunk(at, kka * e_neg, kd * e_neg, rt, kka * e_tail, kd * e_tail, v_ref[rows, :],
                                  jnp.exp(cum_c), [h_scr[p] for p in range(pairs)], masks, pairs)
        for p in range(pairs):
            o_ref[rows, p * LANES:(p + 1) * LANES] = outs[p]
            h_scr[p] = h_new[p]
        return carry

    lax.fori_loop(0, chunks, chunk_body, 0)


def _scan(r, v, kk, kka, kd, ld, *, reverse, seq_len, tb):
    n, rdim = r.shape
    n_seq = n // seq_len
    bps = seq_len // tb
    pairs = rdim // LANES

    def idx(s, j):
        jj = (bps - 1 - j) if reverse else j
        return (s * bps + jj, 0)

    spec = pl.BlockSpec((tb, rdim), idx)
    return pl.pallas_call(
        functools.partial(_scan_kernel, reverse=reverse, pairs=pairs, chunks=tb // CHUNK),
        grid=(n_seq, bps),
        in_specs=[spec] * 6,
        out_specs=spec,
        out_shape=jax.ShapeDtypeStruct((n, rdim), F32),
        scratch_shapes=[pltpu.VMEM((pairs, LANES, LANES), F32)],
        compiler_params=_params(("parallel", "arbitrary")),
        name="scan_bwd" if reverse else "scan_fwd",
    )(r, v, kk, kka, kd, ld)


def _post_kernel(of_ref, ob_ref, bonus_ref, g_ref, yc_ref, x_ref, lnw_ref, lnb_ref, wc_ref, wr_ref, o_ref):
    ones_bd = _head_ones()
    o = of_ref[...] + ob_ref[...]
    mean = _head_sum(o, ones_bd) * (1.0 / HEAD_SIZE)
    d = o - mean
    var = _head_sum(d * d, ones_bd) * (1.0 / HEAD_SIZE)
    y = d * lax.rsqrt(var + GN_EPS) * lnw_ref[...] + lnb_ref[...] + bonus_ref[...]
    y = y * g_ref[...]
    o_ref[...] = x_ref[...] + _mm(yc_ref[...], wc_ref[...]) + _mm(y, wr_ref[...])


def _post(o_f, o_b, bonus, g, yc, x, ln_w, ln_b, w_conv, w_rwkv, *, tm):
    n, d = x.shape
    cdim = yc.shape[1]
    rdim = o_f.shape[1]
    rspec = pl.BlockSpec((tm, rdim), lambda i: (i, 0))
    full = lambda a: pl.BlockSpec(a.shape, lambda i: (0,) * a.ndim)
    return pl.pallas_call(
        _post_kernel,
        grid=(n // tm,),
        in_specs=[rspec, rspec, rspec, rspec, pl.BlockSpec((tm, cdim), lambda i: (i, 0)),
                  pl.BlockSpec((tm, d), lambda i: (i, 0)), full(ln_w), full(ln_b), full(w_conv), full(w_rwkv)],
        out_specs=pl.BlockSpec((tm, d), lambda i: (i, 0)),
        out_shape=jax.ShapeDtypeStruct((n, d), F32),
        compiler_params=_params(("parallel",)),
        name="post",
    )(o_f, o_b, bonus, g, yc, x, ln_w, ln_b, w_conv, w_rwkv)


def _tile(n, pref):
    t = min(pref, n)
    assert n % t == 0, (n, t)
    return t


def _pad_rows(w, rows_before, rows_total):
    return jnp.pad(w, ((rows_before, rows_total - rows_before - w.shape[0]), (0, 0)))


def _layer(x, seq_len, p):
    n, d = x.shape
    cdim = p["conv_w"].shape[1]
    rdim = p["k_k"].shape[1]
    tm = _tile(seq_len, 512)
    dff = p["ffn1_wg"].shape[1]
    tf = _tile(dff, 512)
    x1 = _ffn(x, p["ffn1_norm"], p["ffn1_wg"], p["ffn1_wu"], p["ffn1_wd"], p["final_norm"],
              final=False, tm=tm, tf=tf)
    in_cols = p["w_in"].shape[1]
    tn = in_cols // 4 if in_cols % (4 * LANES) == 0 else in_cols
    z = _inproj(x1, p["mix_norm"], p["w_in"], tm=tm, tn=tn)
    tp = _tile(seq_len, 256)
    (yc, r, v, kk, kka_f, kka_b, k_f, k_b, ld_f, ld_b, bonus, g) = _prep(
        z, p["mu"], p["conv_w"], p["w0"], p["a0"], p["k_k"], p["k_a"], p["r_k"],
        p["w2f"], p["w2b"], p["a2f"], p["a2b"], p["g2"], seq_len=seq_len, tm=tp, cdim=cdim, rdim=rdim)
    tb = _tile(seq_len, 256)
    o_f = _scan(r, v, kk, kka_f, k_f, ld_f, reverse=False, seq_len=seq_len, tb=tb)
    o_b = _scan(r, v, kk, kka_b, k_b, ld_b, reverse=True, seq_len=seq_len, tb=tb)
    x2 = _post(o_f, o_b, bonus, g, yc, x1, p["ln_w"], p["ln_b"], p["w_out_c"], p["w_out_r"], tm=tp)
    return _ffn(x2, p["ffn2_norm"], p["ffn2_wg"], p["ffn2_wu"], p["ffn2_wd"], p["final_norm"],
                final=True, tm=tm, tf=tf)


def kernel(x_prompt, x_sample, ffn1_norm, ffn1_w_gate, ffn1_w_up, ffn1_w_down, mix_norm, w_in, conv_w, mu_shift,
           w0, w2, a0, a2, g2, k_k, k_a, r_k, ln_x_w, ln_x_b, w_out, ffn2_norm, ffn2_w_gate, ffn2_w_up,
           ffn2_w_down, final_norm):
    assert ffn1_norm.shape[0] == 1, "single-layer trunk"
    d = x_prompt.shape[-1]
    cdim = conv_w.shape[-1]
    rdim = k_k.shape[-1]
    in_cols = w_in.shape[-1]
    rwkv_cols = in_cols - 3 * cdim
    assert rwkv_cols == 3 * rdim + LORA_COLS + GATE_LORA
    pad_cols = 3 * rdim + LORA_COLS + GATE_PAD - rwkv_cols
    row = lambda a: a.reshape(1, -1).astype(F32)
    bf = lambda a: a.astype(BF16)
    p = {
        "ffn1_norm": row(ffn1_norm[0]), "ffn1_wg": bf(ffn1_w_gate[0]), "ffn1_wu": bf(ffn1_w_up[0]),
        "ffn1_wd": bf(ffn1_w_down[0]),
        "ffn2_norm": row(ffn2_norm[0]), "ffn2_wg": bf(ffn2_w_gate[0]), "ffn2_wu": bf(ffn2_w_up[0]),
        "ffn2_wd": bf(ffn2_w_down[0]),
        "final_norm": row(final_norm), "mix_norm": row(mix_norm[0]),
        "w_in": bf(jnp.pad(w_in[0], ((0, 0), (0, pad_cols)))),
        "mu": jnp.pad(mu_shift[0], (0, pad_cols)).reshape(1, -1),
        "conv_w": conv_w[0], "w0": w0[0], "a0": a0[0],
        "k_k": row(k_k[0]), "k_a": row(k_a[0]), "r_k": row(r_k[0]),
        "w2f": bf(_pad_rows(w2[0, 0], 0, 2 * DECAY_LORA)), "w2b": bf(_pad_rows(w2[0, 1], DECAY_LORA, 2 * DECAY_LORA)),
        "a2f": bf(_pad_rows(a2[0, 0], 0, 2 * AAA_LORA)), "a2b": bf(_pad_rows(a2[0, 1], AAA_LORA, 2 * AAA_LORA)),
        "g2": bf(_pad_rows(g2[0], 0, GATE_PAD)),
        "ln_w": row(ln_x_w[0]), "ln_b": row(ln_x_b[0]),
        "w_out_c": bf(w_out[0, :cdim]), "w_out_r": bf(w_out[0, cdim:]),
    }
    outs = []
    for x in (x_prompt, x_sample):
        b, t, _ = x.shape
        outs.append(_layer(x.reshape(b * t, d), t, p).reshape(b, t, d))
    return tuple(outs)
```

```python
import functools

import jax
import jax.numpy as jnp
from jax import lax
from jax.experimental import pallas as pl
from jax.experimental.pallas import tpu as pltpu

F32 = jnp.float32
BF16 = jnp.bfloat16
HIGHEST = lax.Precision.HIGHEST

HEAD_SIZE = 64
LANES = 128
SUBLANES = 8
CHUNK = 64
DECAY_LORA = 64
AAA_LORA = 64
GATE_LORA = 160
LORA_COLS = 2 * DECAY_LORA + 2 * AAA_LORA
GATE_PAD = 256
FFN_RESIDUAL_SCALE = 0.5
RMS_EPS = 1e-6
GN_EPS = 64e-5
L2_EPS = 1e-12
DECAY_SCALE = 0.6065306597126334
VMEM_LIMIT = 56 * 1024 * 1024


def _params(sem):
    return pltpu.CompilerParams(dimension_semantics=sem, vmem_limit_bytes=VMEM_LIMIT)


def _rms(x, w):
    return x * lax.rsqrt(jnp.mean(x * x, axis=-1, keepdims=True) + RMS_EPS) * w


def _sigmoid(x):
    return 1.0 / (1.0 + jnp.exp(-x))


def _mm(a, b):
    return jnp.dot(a.astype(BF16), b.astype(BF16), preferred_element_type=F32)


def _ffn_kernel(x_ref, nw_ref, wg_ref, wu_ref, wd_ref, fw_ref, o_ref, h_scr, acc_scr, *, final):
    j = pl.program_id(1)

    @pl.when(j == 0)
    def _():
        h_scr[...] = _rms(x_ref[...], nw_ref[...]).astype(BF16)
        acc_scr[...] = jnp.zeros_like(acc_scr)

    h = h_scr[...]
    g = jnp.dot(h, wg_ref[...], preferred_element_type=F32)
    u = jnp.dot(h, wu_ref[...], preferred_element_type=F32)
    a = (g * _sigmoid(g)) * u
    acc_scr[...] += jnp.dot(a.astype(BF16), wd_ref[...], preferred_element_type=F32)

    @pl.when(j == pl.num_programs(1) - 1)
    def _():
        y = x_ref[...] + FFN_RESIDUAL_SCALE * acc_scr[...]
        if final:
            y = _rms(y, fw_ref[...])
        o_ref[...] = y


def _ffn(x, norm_w, wg, wu, wd, final_w, *, final, tm, tf):
    n, d = x.shape
    dff = wg.shape[1]
    return pl.pallas_call(
        functools.partial(_ffn_kernel, final=final),
        grid=(n // tm, dff // tf),
        in_specs=[
            pl.BlockSpec((tm, d), lambda i, j: (i, 0)),
            pl.BlockSpec((1, d), lambda i, j: (0, 0)),
            pl.BlockSpec((d, tf), lambda i, j: (0, j)),
            pl.BlockSpec((d, tf), lambda i, j: (0, j)),
            pl.BlockSpec((tf, d), lambda i, j: (j, 0)),
            pl.BlockSpec((1, d), lambda i, j: (0, 0)),
        ],
        out_specs=pl.BlockSpec((tm, d), lambda i, j: (i, 0)),
        out_shape=jax.ShapeDtypeStruct((n, d), F32),
        scratch_shapes=[pltpu.VMEM((tm, d), BF16), pltpu.VMEM((tm, d), F32)],
        compiler_params=_params(("parallel", "arbitrary")),
        name="ffn_final" if final else "ffn",
    )(x, norm_w, wg, wu, wd, final_w)


def _head_ones():
    r = lax.broadcasted_iota(jnp.int32, (LANES, LANES), 0) // HEAD_SIZE
    c = lax.broadcasted_iota(jnp.int32, (LANES, LANES), 1) // HEAD_SIZE
    return (r == c).astype(F32)


def _head_sum(x, ones_bd):
    groups = x.shape[1] // LANES
    parts = [jnp.dot(x[:, i * LANES:(i + 1) * LANES], ones_bd, precision=HIGHEST,
                     preferred_element_type=F32) for i in range(groups)]
    return parts[0] if groups == 1 else jnp.concatenate(parts, axis=1)


def _shift_rows(x, prev_row, next_row):
    rows = x.shape[0]
    ridx = lax.broadcasted_iota(jnp.int32, x.shape, 0)
    dn = jnp.where(ridx == 0, prev_row, pltpu.roll(x, 1, axis=0))
    up = jnp.where(ridx == rows - 1, next_row, pltpu.roll(x, rows - 1, axis=0))
    return dn, up


def _halo_specs(tm, d, n):
    sub_per_tile = tm // SUBLANES
    last_sub = n // SUBLANES - 1
    return [
        pl.BlockSpec((tm, d), lambda i: (i, 0)),
        pl.BlockSpec((SUBLANES, d), lambda i: (jnp.maximum(i * sub_per_tile - 1, 0), 0)),
        pl.BlockSpec((SUBLANES, d), lambda i: (jnp.minimum((i + 1) * sub_per_tile, last_sub), 0)),
    ]


def _resident(a):
    return pl.BlockSpec(a.shape, lambda i: (0,) * a.ndim, pipeline_mode=pl.Buffered(1))


def _normed_with_halo(x_ref, xp_ref, xn_ref, nw_ref):
    x_all = jnp.concatenate([x_ref[...], xp_ref[...], xn_ref[...]], axis=0)
    return _rms(x_all, nw_ref[...]).astype(BF16)


def _halo_flags(tiles_per_seq):
    i = pl.program_id(0)
    has_prev = (i % tiles_per_seq != 0).astype(F32)
    has_next = (i % tiles_per_seq != tiles_per_seq - 1).astype(F32)
    return has_prev, has_next


def _split_halo(z, has_prev, has_next):
    tm = z.shape[0] - 2 * SUBLANES
    prev_at = tm + SUBLANES - 1
    next_at = tm + SUBLANES
    return z[:tm], z[prev_at:prev_at + 1] * has_prev, z[next_at:next_at + 1] * has_next


def _conv_kernel(x_ref, xp_ref, xn_ref, nw_ref, w_ref, cw_ref, yc_ref, *, tiles_per_seq, cdim):
    has_prev, has_next = _halo_flags(tiles_per_seq)
    h = _normed_with_halo(x_ref, xp_ref, xn_ref, nw_ref)
    tm = x_ref.shape[0]
    proj = lambda rows, lo: jnp.dot(rows, w_ref[:, lo:lo + cdim], preferred_element_type=F32)
    b_gate = proj(h[:tm], 0)
    u, u_prev, u_next = _split_halo(proj(h, cdim) * proj(h, 2 * cdim), has_prev, has_next)
    u_dn, u_up = _shift_rows(u, u_prev, u_next)
    yc_ref[...] = b_gate * (cw_ref[0:1, :] * u_dn + cw_ref[1:2, :] * u + cw_ref[2:3, :] * u_up)


def _conv_mix(x, norm_w, w_c, conv_w, *, seq_len, tm):
    n, d = x.shape
    cdim = conv_w.shape[1]
    return pl.pallas_call(
        functools.partial(_conv_kernel, tiles_per_seq=seq_len // tm, cdim=cdim),
        grid=(n // tm,),
        in_specs=_halo_specs(tm, d, n) + [_resident(norm_w), _resident(w_c), _resident(conv_w)],
        out_specs=pl.BlockSpec((tm, cdim), lambda i: (i, 0)),
        out_shape=jax.ShapeDtypeStruct((n, cdim), F32),
        compiler_params=_params(("parallel",)),
        name="conv_mix",
    )(x, x, x, norm_w, w_c, conv_w)


def _rwkv_prep_kernel(x_ref, xp_ref, xn_ref, nw_ref, w_ref, mu_ref, w0_ref, a0_ref, kk_ref, ka_ref, rk_ref,
                      w2f_ref, w2b_ref, a2f_ref, a2b_ref, g2_ref,
                      r_ref, v_ref, kkn_ref, kkaf_ref, kkab_ref, kf_ref, kb_ref,
                      ldf_ref, ldb_ref, bonus_ref, g_ref, *, tiles_per_seq, rdim):
    has_prev, has_next = _halo_flags(tiles_per_seq)
    h = _normed_with_halo(x_ref, xp_ref, xn_ref, nw_ref)

    def shifted(lo, hi):
        z = jnp.dot(h, w_ref[:, lo:hi], preferred_element_type=F32)
        zc, z_prev, z_next = _split_halo(z, has_prev, has_next)
        dn, up = _shift_rows(zc, z_prev, z_next)
        return zc + mu_ref[:, lo:hi] * (0.5 * (dn + up) - zc)

    zl = shifted(3 * rdim, 3 * rdim + LORA_COLS)
    zg = shifted(3 * rdim + LORA_COLS, 3 * rdim + LORA_COLS + GATE_PAD)
    g_ref[...] = _mm(_sigmoid(zg), g2_ref[...])
    th = jnp.tanh(zl[:, 0:2 * DECAY_LORA])
    za = zl[:, 2 * DECAY_LORA:LORA_COLS]
    a_dirs = []
    for d, (w2_ref, a2_ref, ld_ref) in enumerate(((w2f_ref, a2f_ref, ldf_ref), (w2b_ref, a2b_ref, ldb_ref))):
        y = w0_ref[d:d + 1, :] + _mm(th, w2_ref[...])
        ld_ref[...] = -DECAY_SCALE * _sigmoid(y)
        a_dirs.append(_sigmoid(a0_ref[d:d + 1, :] + _mm(za, a2_ref[...])))

    ones_bd = _head_ones()
    k = shifted(rdim, 2 * rdim)
    kk = k * kk_ref[...]
    nrm = jnp.sqrt(_head_sum(kk * kk, ones_bd))
    kkn = kk / jnp.maximum(nrm, L2_EPS)
    kkn_ref[...] = kkn
    r = shifted(0, rdim)
    r_ref[...] = r
    rk_sum = jnp.zeros_like(r)
    for a, kka_ref, kd_ref in zip(a_dirs, (kkaf_ref, kkab_ref), (kf_ref, kb_ref)):
        k_d = k * (1.0 + (a - 1.0) * ka_ref[...])
        kka_ref[...] = kkn * a
        kd_ref[...] = k_d
        rk_sum = rk_sum + r * k_d * rk_ref[...]
    v = shifted(2 * rdim, 3 * rdim)
    v_ref[...] = v
    bonus_ref[...] = _head_sum(rk_sum, ones_bd) * v


def _rwkv_prep(x, norm_w, w_r, mu, w0, a0, k_k, k_a, r_k, w2f, w2b, a2f, a2b, g2p, *, seq_len, tm):
    n, d = x.shape
    rdim = k_k.shape[1]
    consts = (norm_w, w_r, mu, w0, a0, k_k, k_a, r_k, w2f, w2b, a2f, a2b, g2p)
    return pl.pallas_call(
        functools.partial(_rwkv_prep_kernel, tiles_per_seq=seq_len // tm, rdim=rdim),
        grid=(n // tm,),
        in_specs=_halo_specs(tm, d, n) + [_resident(a) for a in consts],
        out_specs=[pl.BlockSpec((tm, rdim), lambda i: (i, 0))] * 11,
        out_shape=[jax.ShapeDtypeStruct((n, rdim), F32)] * 11,
        compiler_params=_params(("parallel",)),
        name="rwkv_prep",
    )(x, x, x, *consts)


def _block_diag(y, head0):
    return jnp.concatenate([jnp.where(head0, y, 0.0), jnp.where(head0, 0.0, y)], axis=0)


def _scan_chunk(at, bt, kt, rt, bh, kh, v, decay_c, h_list, masks, pairs):
    head0, strict, incl, eye, bd_mask, diag_mask = masks
    c = CHUNK
    rng = range(pairs)
    sl = [slice(p * LANES, (p + 1) * LANES) for p in rng]
    bd = lambda y: _block_diag(y, head0)

    a_all = [lax.dot_general(
        jnp.concatenate([at[:, sl[p]], rt[:, sl[p]]], axis=0).astype(BF16),
        jnp.concatenate([bd(bt[:, sl[p]]), bd(kt[:, sl[p]])], axis=0).astype(BF16),
        (((1,), (1,)), ((), ())), preferred_element_type=F32) for p in rng]
    a_ab = [jnp.where(strict, a_all[p][:c, :LANES], 0.0) for p in rng]
    a_ak = [jnp.where(strict, a_all[p][:c, LANES:], 0.0) for p in rng]
    a_rb = [jnp.where(incl, a_all[p][c:, :LANES], 0.0) for p in rng]
    a_rk = [jnp.where(incl, a_all[p][c:, LANES:], 0.0) for p in rng]

    t_inv = [eye + a_ab[p] for p in rng]
    a_pow = [_mm(a_ab[p], bd(a_ab[p])) for p in rng]
    for _ in range(CHUNK.bit_length() - 3):
        both = [_mm(jnp.concatenate([t_inv[p], a_pow[p]], axis=0), bd(a_pow[p])) for p in rng]
        t_inv = [t_inv[p] + both[p][:c] for p in rng]
        a_pow = [both[p][c:] for p in rng]
    t_inv = [t_inv[p] + _mm(t_inv[p], bd(a_pow[p])) for p in rng]

    xo = [_mm(jnp.concatenate([a_ak[p], a_rk[p]], axis=0), bd(v[:, sl[p]])) for p in rng]
    x = [xo[p][:c] for p in rng]
    ap_w = [_mm(t_inv[p], jnp.concatenate([bd(at[:, sl[p]]), bd(x[p])], axis=1)) for p in rng]
    q_o = [_mm(a_rb[p], jnp.concatenate([bd(ap_w[p][:, :LANES]), bd(ap_w[p][:, LANES:])], axis=1)) for p in rng]
    lt = [jnp.transpose(jnp.concatenate([bh[:, sl[p]], kh[:, sl[p]]], axis=0)) for p in rng]
    zero = jnp.zeros((c, LANES), F32)
    mn = [_mm(lt[p], jnp.concatenate([ap_w[p], jnp.concatenate([zero, v[:, sl[p]]], axis=1)], axis=0))
          for p in rng]

    outs, h_new = [], []
    for p in rng:
        qp = rt[:, sl[p]] + q_o[p][:, :LANES]
        m_bd = jnp.where(bd_mask, mn[p][:, :LANES], 0.0) + jnp.where(diag_mask, decay_c[:, sl[p]], 0.0)
        n_bd = jnp.where(bd_mask, mn[p][:, LANES:], 0.0)
        oh = _mm(jnp.concatenate([qp, m_bd], axis=0), h_list[p])
        outs.append(oh[:c] + q_o[p][:, LANES:] + xo[p][c:])
        h_new.append(oh[c:] + n_bd)
    return outs, h_new


def _scan_kernel(r_ref, v_ref, kk_ref, kka_ref, kd_ref, ld_ref, o_ref, h_scr, *, reverse, pairs, chunks):
    c = CHUNK

    @pl.when(pl.program_id(1) == 0)
    def _():
        h_scr[...] = jnp.zeros_like(h_scr)

    t_i = lax.broadcasted_iota(jnp.int32, (c, c), 0)
    s_i = lax.broadcasted_iota(jnp.int32, (c, c), 1)
    tri = ((s_i >= t_i) if reverse else (s_i <= t_i)).astype(F32)
    tp = lax.broadcasted_iota(jnp.int32, (c, LANES), 0)
    lane = lax.broadcasted_iota(jnp.int32, (c, LANES), 1)
    sp = lane % HEAD_SIZE
    head0 = lane < HEAD_SIZE
    strict = (sp > tp) if reverse else (sp < tp)
    incl = (sp >= tp) if reverse else (sp <= tp)
    eye = (sp == tp).astype(F32)
    r128 = lax.broadcasted_iota(jnp.int32, (LANES, LANES), 0)
    c128 = lax.broadcasted_iota(jnp.int32, (LANES, LANES), 1)
    bd_mask = (r128 < HEAD_SIZE) == (c128 < HEAD_SIZE)
    diag_mask = r128 == c128
    masks = (head0, strict, incl, eye, bd_mask, diag_mask)
    last = 0 if reverse else c - 1

    def chunk_body(ci, carry):
        cc = (chunks - 1 - ci) if reverse else ci
        rows = pl.ds(pl.multiple_of(cc * c, c), c)
        ld = ld_ref[rows, :]
        cum = jnp.dot(tri, ld, precision=HIGHEST, preferred_element_type=F32)
        cum_c = cum[last:last + 1, :]
        e_neg = jnp.exp(-cum)
        e_tail = jnp.exp(cum_c - cum)
        kk = kk_ref[rows, :]
        kka = kka_ref[rows, :]
        kd = kd_ref[rows, :]
        at = -kk * jnp.exp(cum - ld)
        rt = r_ref[rows, :] * jnp.exp(cum)
        outs, h_new = _scan_chunk(at, kka * e_neg, kd * e_neg, rt, kka * e_tail, kd * e_tail, v_ref[rows, :],
                                  jnp.exp(cum_c), [h_scr[p] for p in range(pairs)], masks, pairs)
        for p in range(pairs):
            o_ref[rows, p * LANES:(p + 1) * LANES] = outs[p]
            h_scr[p] = h_new[p]
        return carry

    lax.fori_loop(0, chunks, chunk_body, 0)


def _scan(r, v, kk, kka, kd, ld, *, reverse, seq_len, tb):
    n, rdim = r.shape
    n_seq = n // seq_len
    bps = seq_len // tb
    pairs = rdim // LANES

    def idx(s, j):
        jj = (bps - 1 - j) if reverse else j
        return (s * bps + jj, 0)

    spec = pl.BlockSpec((tb, rdim), idx)
    return pl.pallas_call(
        functools.partial(_scan_kernel, reverse=reverse, pairs=pairs, chunks=tb // CHUNK),
        grid=(n_seq, bps),
        in_specs=[spec] * 6,
        out_specs=spec,
        out_shape=jax.ShapeDtypeStruct((n, rdim), F32),
        scratch_shapes=[pltpu.VMEM((pairs, LANES, LANES), F32)],
        compiler_params=_params(("parallel", "arbitrary")),
        name="scan_bwd" if reverse else "scan_fwd",
    )(r, v, kk, kka, kd, ld)


def _post_kernel(of_ref, ob_ref, bonus_ref, g_ref, yc_ref, x_ref, lnw_ref, lnb_ref, wc_ref, wr_ref, o_ref):
    ones_bd = _head_ones()
    o = of_ref[...] + ob_ref[...]
    mean = _head_sum(o, ones_bd) * (1.0 / HEAD_SIZE)
    d = o - mean
    var = _head_sum(d * d, ones_bd) * (1.0 / HEAD_SIZE)
    y = d * lax.rsqrt(var + GN_EPS) * lnw_ref[...] + lnb_ref[...] + bonus_ref[...]
    y = y * g_ref[...]
    o_ref[...] = x_ref[...] + _mm(yc_ref[...], wc_ref[...]) + _mm(y, wr_ref[...])


def _post(o_f, o_b, bonus, g, yc, x, ln_w, ln_b, w_conv, w_rwkv, *, tm):
    n, d = x.shape
    cdim = yc.shape[1]
    rdim = o_f.shape[1]
    rspec = pl.BlockSpec((tm, rdim), lambda i: (i, 0))
    full = lambda a: pl.BlockSpec(a.shape, lambda i: (0,) * a.ndim)
    return pl.pallas_call(
        _post_kernel,
        grid=(n // tm,),
        in_specs=[rspec, rspec, rspec, rspec, pl.BlockSpec((tm, cdim), lambda i: (i, 0)),
                  pl.BlockSpec((tm, d), lambda i: (i, 0)), full(ln_w), full(ln_b), full(w_conv), full(w_rwkv)],
        out_specs=pl.BlockSpec((tm, d), lambda i: (i, 0)),
        out_shape=jax.ShapeDtypeStruct((n, d), F32),
        compiler_params=_params(("parallel",)),
        name="post",
    )(o_f, o_b, bonus, g, yc, x, ln_w, ln_b, w_conv, w_rwkv)


def _tile(n, pref):
    t = min(pref, n)
    assert n % t == 0, (n, t)
    return t


def _pad_rows(w, rows_before, rows_total):
    return jnp.pad(w, ((rows_before, rows_total - rows_before - w.shape[0]), (0, 0)))


def _layer(x, seq_len, p):
    n, d = x.shape
    cdim = p["conv_w"].shape[1]
    rdim = p["k_k"].shape[1]
    tm = _tile(seq_len, 512)
    dff = p["ffn1_wg"].shape[1]
    tf = _tile(dff, 512)
    x1 = _ffn(x, p["ffn1_norm"], p["ffn1_wg"], p["ffn1_wu"], p["ffn1_wd"], p["final_norm"],
              final=False, tm=tm, tf=tf)
    tp = _tile(seq_len, 256)
    yc = _conv_mix(x1, p["mix_norm"], p["w_in_c"], p["conv_w"], seq_len=seq_len, tm=tm)
    (r, v, kk, kka_f, kka_b, k_f, k_b, ld_f, ld_b, bonus, g) = _rwkv_prep(
        x1, p["mix_norm"], p["w_in_r"], p["mu"], p["w0"], p["a0"], p["k_k"], p["k_a"], p["r_k"],
        p["w2f"], p["w2b"], p["a2f"], p["a2b"], p["g2"], seq_len=seq_len, tm=tp)
    tb = _tile(seq_len, 256)
    o_f = _scan(r, v, kk, kka_f, k_f, ld_f, reverse=False, seq_len=seq_len, tb=tb)
    o_b = _scan(r, v, kk, kka_b, k_b, ld_b, reverse=True, seq_len=seq_len, tb=tb)
    x2 = _post(o_f, o_b, bonus, g, yc, x1, p["ln_w"], p["ln_b"], p["w_out_c"], p["w_out_r"], tm=tp)
    return _ffn(x2, p["ffn2_norm"], p["ffn2_wg"], p["ffn2_wu"], p["ffn2_wd"], p["final_norm"],
                final=True, tm=tm, tf=tf)


def kernel(x_prompt, x_sample, ffn1_norm, ffn1_w_gate, ffn1_w_up, ffn1_w_down, mix_norm, w_in, conv_w, mu_shift,
           w0, w2, a0, a2, g2, k_k, k_a, r_k, ln_x_w, ln_x_b, w_out, ffn2_norm, ffn2_w_gate, ffn2_w_up,
           ffn2_w_down, final_norm):
    assert ffn1_norm.shape[0] == 1, "single-layer trunk"
    d = x_prompt.shape[-1]
    cdim = conv_w.shape[-1]
    rdim = k_k.shape[-1]
    in_cols = w_in.shape[-1]
    rwkv_cols = in_cols - 3 * cdim
    assert rwkv_cols == 3 * rdim + LORA_COLS + GATE_LORA
    pad_cols = 3 * rdim + LORA_COLS + GATE_PAD - rwkv_cols
    row = lambda a: a.reshape(1, -1).astype(F32)
    bf = lambda a: a.astype(BF16)
    p = {
        "ffn1_norm": row(ffn1_norm[0]), "ffn1_wg": bf(ffn1_w_gate[0]), "ffn1_wu": bf(ffn1_w_up[0]),
        "ffn1_wd": bf(ffn1_w_down[0]),
        "ffn2_norm": row(ffn2_norm[0]), "ffn2_wg": bf(ffn2_w_gate[0]), "ffn2_wu": bf(ffn2_w_up[0]),
        "ffn2_wd": bf(ffn2_w_down[0]),
        "final_norm": row(final_norm), "mix_norm": row(mix_norm[0]),
        "w_in_c": bf(w_in[0, :, :3 * cdim]),
        "w_in_r": bf(jnp.pad(w_in[0, :, 3 * cdim:], ((0, 0), (0, pad_cols)))),
        "mu": jnp.pad(mu_shift[0], (0, pad_cols)).reshape(1, -1),
        "conv_w": conv_w[0], "w0": w0[0], "a0": a0[0],
        "k_k": row(k_k[0]), "k_a": row(k_a[0]), "r_k": row(r_k[0]),
        "w2f": bf(_pad_rows(w2[0, 0], 0, 2 * DECAY_LORA)), "w2b": bf(_pad_rows(w2[0, 1], DECAY_LORA, 2 * DECAY_LORA)),
        "a2f": bf(_pad_rows(a2[0, 0], 0, 2 * AAA_LORA)), "a2b": bf(_pad_rows(a2[0, 1], AAA_LORA, 2 * AAA_LORA)),
        "g2": bf(_pad_rows(g2[0], 0, GATE_PAD)),
        "ln_w": row(ln_x_w[0]), "ln_b": row(ln_x_b[0]),
        "w_out_c": bf(w_out[0, :cdim]), "w_out_r": bf(w_out[0, cdim:]),
    }
    outs = []
    for x in (x_prompt, x_sample):
        b, t, _ = x.shape
        outs.append(_layer(x.reshape(b * t, d), t, p).reshape(b, t, d))
    return tuple(outs)
```

```python
import functools

import jax
import jax.numpy as jnp
from jax import lax
from jax.experimental import pallas as pl
from jax.experimental.pallas import tpu as pltpu

F32 = jnp.float32
BF16 = jnp.bfloat16

HEAD_SIZE = 64
LANES = 128
MXU_WIDTH = 256
SUBLANES = 8
CHUNK = 64
SCAN_GROUP = 2
DECAY_LORA = 64
AAA_LORA = 64
GATE_LORA = 160
LORA_COLS = 2 * DECAY_LORA + 2 * AAA_LORA
GATE_PAD = 256
FFN_RESIDUAL_SCALE = 0.5
RMS_EPS = 1e-6
GN_EPS = 64e-5
L2_EPS = 1e-12
DECAY_SCALE = 0.6065306597126334
VMEM_LIMIT = 56 * 1024 * 1024


def _params(sem):
    return pltpu.CompilerParams(dimension_semantics=sem, vmem_limit_bytes=VMEM_LIMIT)


def _rms(x, w):
    return x * lax.rsqrt(jnp.mean(x * x, axis=-1, keepdims=True) + RMS_EPS) * w


def _sigmoid(x):
    return 1.0 / (1.0 + jnp.exp(-x))


def _mm(a, b):
    return jnp.dot(a.astype(BF16), b.astype(BF16), preferred_element_type=F32)


def _ffn_kernel(x_ref, nw_ref, wg_ref, wu_ref, wd_ref, fw_ref, o_ref, h_scr, acc_scr, *, final):
    j = pl.program_id(1)

    @pl.when(j == 0)
    def _():
        h_scr[...] = _rms(x_ref[...], nw_ref[...]).astype(BF16)
        acc_scr[...] = jnp.zeros_like(acc_scr)

    h = h_scr[...]
    g = jnp.dot(h, wg_ref[...], preferred_element_type=F32)
    u = jnp.dot(h, wu_ref[...], preferred_element_type=F32)
    a = (g * _sigmoid(g)) * u
    acc_scr[...] += jnp.dot(a.astype(BF16), wd_ref[...], preferred_element_type=F32)

    @pl.when(j == pl.num_programs(1) - 1)
    def _():
        y = x_ref[...] + FFN_RESIDUAL_SCALE * acc_scr[...]
        if final:
            y = _rms(y, fw_ref[...])
        o_ref[...] = y


def _ffn(x, norm_w, wg, wu, wd, final_w, *, final, tm, tf):
    n, d = x.shape
    dff = wg.shape[1]
    return pl.pallas_call(
        functools.partial(_ffn_kernel, final=final),
        grid=(n // tm, dff // tf),
        in_specs=[
            pl.BlockSpec((tm, d), lambda i, j: (i, 0)),
            pl.BlockSpec((1, d), lambda i, j: (0, 0)),
            pl.BlockSpec((d, tf), lambda i, j: (0, j)),
            pl.BlockSpec((d, tf), lambda i, j: (0, j)),
            pl.BlockSpec((tf, d), lambda i, j: (j, 0)),
            pl.BlockSpec((1, d), lambda i, j: (0, 0)),
        ],
        out_specs=pl.BlockSpec((tm, d), lambda i, j: (i, 0)),
        out_shape=jax.ShapeDtypeStruct((n, d), F32),
        scratch_shapes=[pltpu.VMEM((tm, d), BF16), pltpu.VMEM((tm, d), F32)],
        compiler_params=_params(("parallel", "arbitrary")),
        name="ffn_final" if final else "ffn",
    )(x, norm_w, wg, wu, wd, final_w)


def _split3(x):
    hi = x.astype(BF16)
    rest = x - hi.astype(F32)
    mid = rest.astype(BF16)
    lo = (rest - mid.astype(F32)).astype(BF16)
    return hi, mid, lo


def _head_ones(width):
    r = lax.broadcasted_iota(jnp.int32, (width, width), 0) // HEAD_SIZE
    c = lax.broadcasted_iota(jnp.int32, (width, width), 1) // HEAD_SIZE
    return (r == c).astype(BF16)


def _head_sum(x):
    width = MXU_WIDTH if x.shape[1] % MXU_WIDTH == 0 else LANES
    ones_bd = _head_ones(width)
    parts = []
    for lo in range(0, x.shape[1], width):
        hi_mid_lo = _split3(x[:, lo:lo + width])
        parts.append(sum(jnp.dot(t, ones_bd, preferred_element_type=F32) for t in hi_mid_lo))
    return parts[0] if len(parts) == 1 else jnp.concatenate(parts, axis=1)


def _shift_rows(x, prev_row, next_row):
    rows = x.shape[0]
    ridx = lax.broadcasted_iota(jnp.int32, x.shape, 0)
    dn = jnp.where(ridx == 0, prev_row, pltpu.roll(x, 1, axis=0))
    up = jnp.where(ridx == rows - 1, next_row, pltpu.roll(x, rows - 1, axis=0))
    return dn, up


def _halo_specs(tm, d, n):
    sub_per_tile = tm // SUBLANES
    last_sub = n // SUBLANES - 1
    return [
        pl.BlockSpec((tm, d), lambda i: (i, 0)),
        pl.BlockSpec((SUBLANES, d), lambda i: (jnp.maximum(i * sub_per_tile - 1, 0), 0)),
        pl.BlockSpec((SUBLANES, d), lambda i: (jnp.minimum((i + 1) * sub_per_tile, last_sub), 0)),
    ]


def _resident(a):
    return pl.BlockSpec(a.shape, lambda i: (0,) * a.ndim, pipeline_mode=pl.Buffered(1))


def _normed_with_halo(x_ref, xp_ref, xn_ref, nw_ref):
    x_all = jnp.concatenate([x_ref[...], xp_ref[...], xn_ref[...]], axis=0)
    return _rms(x_all, nw_ref[...]).astype(BF16)


def _halo_flags(tiles_per_seq):
    i = pl.program_id(0)
    has_prev = (i % tiles_per_seq != 0).astype(F32)
    has_next = (i % tiles_per_seq != tiles_per_seq - 1).astype(F32)
    return has_prev, has_next


def _split_halo(z, has_prev, has_next):
    tm = z.shape[0] - 2 * SUBLANES
    prev_at = tm + SUBLANES - 1
    next_at = tm + SUBLANES
    return z[:tm], z[prev_at:prev_at + 1] * has_prev, z[next_at:next_at + 1] * has_next


def _conv_kernel(x_ref, xp_ref, xn_ref, nw_ref, w_ref, cw_ref, yc_ref, *, tiles_per_seq, cdim):
    has_prev, has_next = _halo_flags(tiles_per_seq)
    h = _normed_with_halo(x_ref, xp_ref, xn_ref, nw_ref)
    tm = x_ref.shape[0]
    proj = lambda rows, lo: jnp.dot(rows, w_ref[:, lo:lo + cdim], preferred_element_type=F32)
    b_gate = proj(h[:tm], 0)
    u, u_prev, u_next = _split_halo(proj(h, cdim) * proj(h, 2 * cdim), has_prev, has_next)
    u_dn, u_up = _shift_rows(u, u_prev, u_next)
    yc_ref[...] = b_gate * (cw_ref[0:1, :] * u_dn + cw_ref[1:2, :] * u + cw_ref[2:3, :] * u_up)


def _conv_mix(x, norm_w, w_c, conv_w, *, seq_len, tm):
    n, d = x.shape
    cdim = conv_w.shape[1]
    return pl.pallas_call(
        functools.partial(_conv_kernel, tiles_per_seq=seq_len // tm, cdim=cdim),
        grid=(n // tm,),
        in_specs=_halo_specs(tm, d, n) + [_resident(norm_w), _resident(w_c), _resident(conv_w)],
        out_specs=pl.BlockSpec((tm, cdim), lambda i: (i, 0)),
        out_shape=jax.ShapeDtypeStruct((n, cdim), F32),
        compiler_params=_params(("parallel",)),
        name="conv_mix",
    )(x, x, x, norm_w, w_c, conv_w)


def _rwkv_prep_kernel(x_ref, xp_ref, xn_ref, nw_ref, w_ref, mu_ref, w0_ref, a0_ref, kk_ref, ka_ref, rk_ref,
                      w2f_ref, w2b_ref, a2f_ref, a2b_ref, g2_ref,
                      r_ref, v_ref, kkn_ref, kkaf_ref, kkab_ref, kf_ref, kb_ref,
                      ldf_ref, ldb_ref, bonus_ref, g_ref, *, tiles_per_seq, rdim):
    has_prev, has_next = _halo_flags(tiles_per_seq)
    h = _normed_with_halo(x_ref, xp_ref, xn_ref, nw_ref)

    def shifted(lo, hi):
        z = jnp.dot(h, w_ref[:, lo:hi], preferred_element_type=F32)
        zc, z_prev, z_next = _split_halo(z, has_prev, has_next)
        dn, up = _shift_rows(zc, z_prev, z_next)
        return zc + mu_ref[:, lo:hi] * (0.5 * (dn + up) - zc)

    zl = shifted(3 * rdim, 3 * rdim + LORA_COLS)
    zg = shifted(3 * rdim + LORA_COLS, 3 * rdim + LORA_COLS + GATE_PAD)
    g_ref[...] = _mm(_sigmoid(zg), g2_ref[...])
    th = jnp.tanh(zl[:, 0:2 * DECAY_LORA])
    za = zl[:, 2 * DECAY_LORA:LORA_COLS]
    a_dirs = []
    for d, (w2_ref, a2_ref, ld_ref) in enumerate(((w2f_ref, a2f_ref, ldf_ref), (w2b_ref, a2b_ref, ldb_ref))):
        y = w0_ref[d:d + 1, :] + _mm(th, w2_ref[...])
        ld_ref[...] = -DECAY_SCALE * _sigmoid(y)
        a_dirs.append(_sigmoid(a0_ref[d:d + 1, :] + _mm(za, a2_ref[...])))

    k = shifted(rdim, 2 * rdim)
    kk = k * kk_ref[...]
    nrm = jnp.sqrt(_head_sum(kk * kk))
    kkn = kk / jnp.maximum(nrm, L2_EPS)
    kkn_ref[...] = kkn
    r = shifted(0, rdim)
    r_ref[...] = r
    rk_sum = jnp.zeros_like(r)
    for a, kka_ref, kd_ref in zip(a_dirs, (kkaf_ref, kkab_ref), (kf_ref, kb_ref)):
        k_d = k * (1.0 + (a - 1.0) * ka_ref[...])
        kka_ref[...] = kkn * a
        kd_ref[...] = k_d
        rk_sum = rk_sum + r * k_d * rk_ref[...]
    v = shifted(2 * rdim, 3 * rdim)
    v_ref[...] = v
    bonus_ref[...] = _head_sum(rk_sum) * v


def _rwkv_prep(x, norm_w, w_r, mu, w0, a0, k_k, k_a, r_k, w2f, w2b, a2f, a2b, g2p, *, seq_len, tm):
    n, d = x.shape
    rdim = k_k.shape[1]
    consts = (norm_w, w_r, mu, w0, a0, k_k, k_a, r_k, w2f, w2b, a2f, a2b, g2p)
    return pl.pallas_call(
        functools.partial(_rwkv_prep_kernel, tiles_per_seq=seq_len // tm, rdim=rdim),
        grid=(n // tm,),
        in_specs=_halo_specs(tm, d, n) + [_resident(a) for a in consts],
        out_specs=[pl.BlockSpec((tm, rdim), lambda i: (i, 0))] * 11,
        out_shape=[jax.ShapeDtypeStruct((n, rdim), F32)] * 11,
        compiler_params=_params(("parallel",)),
        name="rwkv_prep",
    )(x, x, x, *consts)


def _block_diag(y, head0):
    return jnp.concatenate([jnp.where(head0, y, 0.0), jnp.where(head0, 0.0, y)], axis=0)


def _scan_intra(ops, masks):
    head0, strict, incl, eye, bd_mask, diag_mask = masks
    c = CHUNK
    rng = range(len(ops))
    at, bt, kt, rt, bh, kh, v, decay_c = (list(t) for t in zip(*ops))
    bd = lambda y: _block_diag(y, head0)

    a_all = [lax.dot_general(
        jnp.concatenate([at[i], rt[i]], axis=0).astype(BF16),
        jnp.concatenate([bd(bt[i]), bd(kt[i])], axis=0).astype(BF16),
        (((1,), (1,)), ((), ())), preferred_element_type=F32) for i in rng]
    a_ab = [jnp.where(strict, a_all[i][:c, :LANES], 0.0) for i in rng]
    a_ak = [jnp.where(strict, a_all[i][:c, LANES:], 0.0) for i in rng]
    a_rb = [jnp.where(incl, a_all[i][c:, :LANES], 0.0) for i in rng]
    a_rk = [jnp.where(incl, a_all[i][c:, LANES:], 0.0) for i in rng]

    t_inv = [eye + a_ab[i] for i in rng]
    a_pow = [_mm(a_ab[i], bd(a_ab[i])) for i in rng]
    for _ in range(CHUNK.bit_length() - 3):
        both = [_mm(jnp.concatenate([t_inv[i], a_pow[i]], axis=0), bd(a_pow[i])) for i in rng]
        t_inv = [t_inv[i] + both[i][:c] for i in rng]
        a_pow = [both[i][c:] for i in rng]
    t_inv = [t_inv[i] + _mm(t_inv[i], bd(a_pow[i])) for i in rng]

    xo = [_mm(jnp.concatenate([a_ak[i], a_rk[i]], axis=0), bd(v[i])) for i in rng]
    ap_w = [_mm(t_inv[i], jnp.concatenate([bd(at[i]), bd(xo[i][:c])], axis=1)) for i in rng]
    q_o = [_mm(a_rb[i], jnp.concatenate([bd(ap_w[i][:, :LANES]), bd(ap_w[i][:, LANES:])], axis=1)) for i in rng]
    lt = [jnp.transpose(jnp.concatenate([bh[i], kh[i]], axis=0)) for i in rng]
    zero = jnp.zeros((c, LANES), F32)
    mn = [_mm(lt[i], jnp.concatenate([ap_w[i], jnp.concatenate([zero, v[i]], axis=1)], axis=0))
          for i in rng]

    res = []
    for i in rng:
        qp = rt[i] + q_o[i][:, :LANES]
        m_bd = jnp.where(bd_mask, mn[i][:, :LANES], 0.0) + jnp.where(diag_mask, decay_c[i], 0.0)
        n_bd = jnp.where(bd_mask, mn[i][:, LANES:], 0.0)
        res.append((jnp.concatenate([qp, m_bd], axis=0), q_o[i][:, LANES:] + xo[i][c:], n_bd))
    return res


def _scan_kernel(r_ref, v_ref, kk_ref, kka_ref, kd_ref, ld_ref, o_ref, h_scr, *, reverse, pairs, chunks):
    c = CHUNK

    @pl.when(pl.program_id(1) == 0)
    def _():
        h_scr[...] = jnp.zeros_like(h_scr)

    t_i = lax.broadcasted_iota(jnp.int32, (c, c), 0)
    s_i = lax.broadcasted_iota(jnp.int32, (c, c), 1)
    tri = ((s_i >= t_i) if reverse else (s_i <= t_i)).astype(BF16)
    tp = lax.broadcasted_iota(jnp.int32, (c, LANES), 0)
    lane = lax.broadcasted_iota(jnp.int32, (c, LANES), 1)
    sp = lane % HEAD_SIZE
    head0 = lane < HEAD_SIZE
    strict = (sp > tp) if reverse else (sp < tp)
    incl = (sp >= tp) if reverse else (sp <= tp)
    eye = (sp == tp).astype(F32)
    r128 = lax.broadcasted_iota(jnp.int32, (LANES, LANES), 0)
    c128 = lax.broadcasted_iota(jnp.int32, (LANES, LANES), 1)
    bd_mask = (r128 < HEAD_SIZE) == (c128 < HEAD_SIZE)
    diag_mask = r128 == c128
    masks = (head0, strict, incl, eye, bd_mask, diag_mask)
    last = 0 if reverse else c - 1

    group = SCAN_GROUP if chunks % SCAN_GROUP == 0 else 1

    def chunk_operands(rows):
        ld = ld_ref[rows, :]
        cum = sum(jnp.dot(tri, t, preferred_element_type=F32) for t in _split3(ld))
        cum_c = cum[last:last + 1, :]
        e_neg = jnp.exp(-cum)
        e_tail = jnp.exp(cum_c - cum)
        kka = kka_ref[rows, :]
        kd = kd_ref[rows, :]
        full = (-kk_ref[rows, :] * jnp.exp(cum - ld), kka * e_neg, kd * e_neg, r_ref[rows, :] * jnp.exp(cum),
                kka * e_tail, kd * e_tail, v_ref[rows, :], jnp.exp(cum_c))
        return [tuple(t[:, p * LANES:(p + 1) * LANES] for t in full) for p in range(pairs)]

    def group_body(gi, carry):
        rows = []
        for j in range(group):
            ci = gi * group + j
            cc = (chunks - 1 - ci) if reverse else ci
            rows.append(pl.ds(pl.multiple_of(cc * c, c), c))
        ops = [op for rw in rows for op in chunk_operands(rw)]
        intra = _scan_intra(ops, masks)
        h = [h_scr[p] for p in range(pairs)]
        for j in range(group):
            for p in range(pairs):
                lhs, o_intra, n_bd = intra[j * pairs + p]
                oh = _mm(lhs, h[p])
                o_ref[rows[j], p * LANES:(p + 1) * LANES] = oh[:c] + o_intra
                h[p] = oh[c:] + n_bd
        for p in range(pairs):
            h_scr[p] = h[p]
        return carry

    lax.fori_loop(0, chunks // group, group_body, 0)


def _scan(r, v, kk, kka, kd, ld, *, reverse, seq_len, tb):
    n, rdim = r.shape
    n_seq = n // seq_len
    bps = seq_len // tb
    pairs = rdim // LANES

    def idx(s, j):
        jj = (bps - 1 - j) if reverse else j
        return (s * bps + jj, 0)

    spec = pl.BlockSpec((tb, rdim), idx)
    return pl.pallas_call(
        functools.partial(_scan_kernel, reverse=reverse, pairs=pairs, chunks=tb // CHUNK),
        grid=(n_seq, bps),
        in_specs=[spec] * 6,
        out_specs=spec,
        out_shape=jax.ShapeDtypeStruct((n, rdim), F32),
        scratch_shapes=[pltpu.VMEM((pairs, LANES, LANES), F32)],
        compiler_params=_params(("parallel", "arbitrary")),
        name="scan_bwd" if reverse else "scan_fwd",
    )(r, v, kk, kka, kd, ld)


def _post_kernel(of_ref, ob_ref, bonus_ref, g_ref, yc_ref, x_ref, lnw_ref, lnb_ref, wc_ref, wr_ref, o_ref):
    o = of_ref[...] + ob_ref[...]
    mean = _head_sum(o) * (1.0 / HEAD_SIZE)
    d = o - mean
    var = _head_sum(d * d) * (1.0 / HEAD_SIZE)
    y = d * lax.rsqrt(var + GN_EPS) * lnw_ref[...] + lnb_ref[...] + bonus_ref[...]
    y = y * g_ref[...]
    o_ref[...] = x_ref[...] + _mm(yc_ref[...], wc_ref[...]) + _mm(y, wr_ref[...])


def _post(o_f, o_b, bonus, g, yc, x, ln_w, ln_b, w_conv, w_rwkv, *, tm):
    n, d = x.shape
    cdim = yc.shape[1]
    rdim = o_f.shape[1]
    rspec = pl.BlockSpec((tm, rdim), lambda i: (i, 0))
    full = lambda a: pl.BlockSpec(a.shape, lambda i: (0,) * a.ndim)
    return pl.pallas_call(
        _post_kernel,
        grid=(n // tm,),
        in_specs=[rspec, rspec, rspec, rspec, pl.BlockSpec((tm, cdim), lambda i: (i, 0)),
                  pl.BlockSpec((tm, d), lambda i: (i, 0)), full(ln_w), full(ln_b), full(w_conv), full(w_rwkv)],
        out_specs=pl.BlockSpec((tm, d), lambda i: (i, 0)),
        out_shape=jax.ShapeDtypeStruct((n, d), F32),
        compiler_params=_params(("parallel",)),
        name="post",
    )(o_f, o_b, bonus, g, yc, x, ln_w, ln_b, w_conv, w_rwkv)


def _tile(n, pref):
    t = min(pref, n)
    assert n % t == 0, (n, t)
    return t


def _pad_rows(w, rows_before, rows_total):
    return jnp.pad(w, ((rows_before, rows_total - rows_before - w.shape[0]), (0, 0)))


def _layer(x, seq_len, p):
    n, d = x.shape
    cdim = p["conv_w"].shape[1]
    rdim = p["k_k"].shape[1]
    tm = _tile(seq_len, 512)
    dff = p["ffn1_wg"].shape[1]
    tf = _tile(dff, 512)
    x1 = _ffn(x, p["ffn1_norm"], p["ffn1_wg"], p["ffn1_wu"], p["ffn1_wd"], p["final_norm"],
              final=False, tm=tm, tf=tf)
    tp = _tile(seq_len, 256)
    yc = _conv_mix(x1, p["mix_norm"], p["w_in_c"], p["conv_w"], seq_len=seq_len, tm=tm)
    (r, v, kk, kka_f, kka_b, k_f, k_b, ld_f, ld_b, bonus, g) = _rwkv_prep(
        x1, p["mix_norm"], p["w_in_r"], p["mu"], p["w0"], p["a0"], p["k_k"], p["k_a"], p["r_k"],
        p["w2f"], p["w2b"], p["a2f"], p["a2b"], p["g2"], seq_len=seq_len, tm=tp)
    tb = _tile(seq_len, 256)
    o_f = _scan(r, v, kk, kka_f, k_f, ld_f, reverse=False, seq_len=seq_len, tb=tb)
    o_b = _scan(r, v, kk, kka_b, k_b, ld_b, reverse=True, seq_len=seq_len, tb=tb)
    x2 = _post(o_f, o_b, bonus, g, yc, x1, p["ln_w"], p["ln_b"], p["w_out_c"], p["w_out_r"], tm=tp)
    return _ffn(x2, p["ffn2_norm"], p["ffn2_wg"], p["ffn2_wu"], p["ffn2_wd"], p["final_norm"],
                final=True, tm=tm, tf=tf)


def kernel(x_prompt, x_sample, ffn1_norm, ffn1_w_gate, ffn1_w_up, ffn1_w_down, mix_norm, w_in, conv_w, mu_shift,
           w0, w2, a0, a2, g2, k_k, k_a, r_k, ln_x_w, ln_x_b, w_out, ffn2_norm, ffn2_w_gate, ffn2_w_up,
           ffn2_w_down, final_norm):
    assert ffn1_norm.shape[0] == 1, "single-layer trunk"
    d = x_prompt.shape[-1]
    cdim = conv_w.shape[-1]
    rdim = k_k.shape[-1]
    in_cols = w_in.shape[-1]
    rwkv_cols = in_cols - 3 * cdim
    assert rwkv_cols == 3 * rdim + LORA_COLS + GATE_LORA
    pad_cols = 3 * rdim + LORA_COLS + GATE_PAD - rwkv_cols
    row = lambda a: a.reshape(1, -1).astype(F32)
    bf = lambda a: a.astype(BF16)
    p = {
        "ffn1_norm": row(ffn1_norm[0]), "ffn1_wg": bf(ffn1_w_gate[0]), "ffn1_wu": bf(ffn1_w_up[0]),
        "ffn1_wd": bf(ffn1_w_down[0]),
        "ffn2_norm": row(ffn2_norm[0]), "ffn2_wg": bf(ffn2_w_gate[0]), "ffn2_wu": bf(ffn2_w_up[0]),
        "ffn2_wd": bf(ffn2_w_down[0]),
        "final_norm": row(final_norm), "mix_norm": row(mix_norm[0]),
        "w_in_c": bf(w_in[0, :, :3 * cdim]),
        "w_in_r": bf(jnp.pad(w_in[0, :, 3 * cdim:], ((0, 0), (0, pad_cols)))),
        "mu": jnp.pad(mu_shift[0], (0, pad_cols)).reshape(1, -1),
        "conv_w": conv_w[0], "w0": w0[0], "a0": a0[0],
        "k_k": row(k_k[0]), "k_a": row(k_a[0]), "r_k": row(r_k[0]),
        "w2f": bf(_pad_rows(w2[0, 0], 0, 2 * DECAY_LORA)), "w2b": bf(_pad_rows(w2[0, 1], DECAY_LORA, 2 * DECAY_LORA)),
        "a2f": bf(_pad_rows(a2[0, 0], 0, 2 * AAA_LORA)), "a2b": bf(_pad_rows(a2[0, 1], AAA_LORA, 2 * AAA_LORA)),
        "g2": bf(_pad_rows(g2[0], 0, GATE_PAD)),
        "ln_w": row(ln_x_w[0]), "ln_b": row(ln_x_b[0]),
        "w_out_c": bf(w_out[0, :cdim]), "w_out_r": bf(w_out[0, cdim:]),
    }
    outs = []
    for x in (x_prompt, x_sample):
        b, t, _ = x.shape
        outs.append(_layer(x.reshape(b * t, d), t, p).reshape(b, t, d))
    return tuple(outs)
```

```python
import functools

import jax
import jax.numpy as jnp
from jax import lax
from jax.experimental import pallas as pl
from jax.experimental.pallas import tpu as pltpu

F32 = jnp.float32
BF16 = jnp.bfloat16

HEAD_SIZE = 64
LANES = 128
MXU_WIDTH = 256
SUBLANES = 8
CHUNK = 64
FFN_SLAB = 128
FFN_ROWS = 512
SCAN_GROUP = 2
DECAY_LORA = 64
AAA_LORA = 64
GATE_LORA = 160
LORA_COLS = 2 * DECAY_LORA + 2 * AAA_LORA
GATE_PAD = 256
FFN_RESIDUAL_SCALE = 0.5
RMS_EPS = 1e-6
GN_EPS = 64e-5
L2_EPS = 1e-12
DECAY_SCALE = 0.6065306597126334
VMEM_LIMIT = 60 * 1024 * 1024


def _params(sem):
    return pltpu.CompilerParams(dimension_semantics=sem, vmem_limit_bytes=VMEM_LIMIT)


def _rms(x, w):
    return x * lax.rsqrt(jnp.mean(x * x, axis=-1, keepdims=True) + RMS_EPS) * w


def _sigmoid(x):
    return 1.0 / (1.0 + jnp.exp(-x))


def _mm(a, b):
    return jnp.dot(a.astype(BF16), b.astype(BF16), preferred_element_type=F32)


def _ffn_kernel(x_ref, nw_ref, wg_ref, wu_ref, wd_ref, fw_ref, o_ref, h_scr, *, final):
    j = pl.program_id(1)

    slabs = [pl.ds(r, FFN_SLAB) for r in range(0, x_ref.shape[0], FFN_SLAB)]

    @pl.when(j == 0)
    def _():
        for rows in slabs:
            h_scr[rows, :] = _rms(x_ref[rows, :], nw_ref[...]).astype(BF16)
        o_ref[...] = jnp.zeros_like(o_ref)

    half = min(FFN_ROWS, x_ref.shape[0])
    for r in range(0, x_ref.shape[0], half):
        rows = pl.ds(r, half)
        h = h_scr[rows, :]
        g = jnp.dot(h, wg_ref[...], preferred_element_type=F32)
        u = jnp.dot(h, wu_ref[...], preferred_element_type=F32)
        a = ((g * _sigmoid(g)) * u).astype(BF16)
        o_ref[rows, :] += jnp.dot(a, wd_ref[...], preferred_element_type=F32)

    @pl.when(j == pl.num_programs(1) - 1)
    def _():
        for rows in slabs:
            y = x_ref[rows, :] + FFN_RESIDUAL_SCALE * o_ref[rows, :]
            if final:
                y = _rms(y, fw_ref[...])
            o_ref[rows, :] = y


def _ffn(x, norm_w, wg, wu, wd, final_w, *, final, tm, tf):
    n, d = x.shape
    dff = wg.shape[1]
    return pl.pallas_call(
        functools.partial(_ffn_kernel, final=final),
        grid=(n // tm, dff // tf),
        in_specs=[
            pl.BlockSpec((tm, d), lambda i, j: (i, 0)),
            pl.BlockSpec((1, d), lambda i, j: (0, 0)),
            pl.BlockSpec((d, tf), lambda i, j: (0, j)),
            pl.BlockSpec((d, tf), lambda i, j: (0, j)),
            pl.BlockSpec((tf, d), lambda i, j: (j, 0)),
            pl.BlockSpec((1, d), lambda i, j: (0, 0)),
        ],
        out_specs=pl.BlockSpec((tm, d), lambda i, j: (i, 0)),
        out_shape=jax.ShapeDtypeStruct((n, d), F32),
        scratch_shapes=[pltpu.VMEM((tm, d), BF16)],
        compiler_params=_params(("parallel", "arbitrary")),
        name="ffn_final" if final else "ffn",
    )(x, norm_w, wg, wu, wd, final_w)


def _split3(x):
    hi = x.astype(BF16)
    rest = x - hi.astype(F32)
    mid = rest.astype(BF16)
    lo = (rest - mid.astype(F32)).astype(BF16)
    return hi, mid, lo


def _head_ones(width):
    r = lax.broadcasted_iota(jnp.int32, (width, width), 0) // HEAD_SIZE
    c = lax.broadcasted_iota(jnp.int32, (width, width), 1) // HEAD_SIZE
    return (r == c).astype(BF16)


def _head_sum(x):
    width = MXU_WIDTH if x.shape[1] % MXU_WIDTH == 0 else LANES
    ones_bd = _head_ones(width)
    parts = []
    for lo in range(0, x.shape[1], width):
        hi_mid_lo = _split3(x[:, lo:lo + width])
        parts.append(sum(jnp.dot(t, ones_bd, preferred_element_type=F32) for t in hi_mid_lo))
    return parts[0] if len(parts) == 1 else jnp.concatenate(parts, axis=1)


def _shift_rows(x, prev_row, next_row):
    rows = x.shape[0]
    ridx = lax.broadcasted_iota(jnp.int32, x.shape, 0)
    dn = jnp.where(ridx == 0, prev_row, pltpu.roll(x, 1, axis=0))
    up = jnp.where(ridx == rows - 1, next_row, pltpu.roll(x, rows - 1, axis=0))
    return dn, up


def _halo_specs(tm, d, n):
    sub_per_tile = tm // SUBLANES
    last_sub = n // SUBLANES - 1
    return [
        pl.BlockSpec((tm, d), lambda i: (i, 0)),
        pl.BlockSpec((SUBLANES, d), lambda i: (jnp.maximum(i * sub_per_tile - 1, 0), 0)),
        pl.BlockSpec((SUBLANES, d), lambda i: (jnp.minimum((i + 1) * sub_per_tile, last_sub), 0)),
    ]


def _resident(a):
    return pl.BlockSpec(a.shape, lambda i: (0,) * a.ndim, pipeline_mode=pl.Buffered(1))


def _normed_with_halo(x_ref, xp_ref, xn_ref, nw_ref):
    x_all = jnp.concatenate([x_ref[...], xp_ref[...], xn_ref[...]], axis=0)
    return _rms(x_all, nw_ref[...]).astype(BF16)


def _halo_flags(tiles_per_seq):
    i = pl.program_id(0)
    has_prev = (i % tiles_per_seq != 0).astype(F32)
    has_next = (i % tiles_per_seq != tiles_per_seq - 1).astype(F32)
    return has_prev, has_next


def _split_halo(z, has_prev, has_next):
    tm = z.shape[0] - 2 * SUBLANES
    prev_at = tm + SUBLANES - 1
    next_at = tm + SUBLANES
    return z[:tm], z[prev_at:prev_at + 1] * has_prev, z[next_at:next_at + 1] * has_next


def _conv_kernel(x_ref, xp_ref, xn_ref, nw_ref, w_ref, cw_ref, yc_ref, *, tiles_per_seq, cdim):
    has_prev, has_next = _halo_flags(tiles_per_seq)
    h = _normed_with_halo(x_ref, xp_ref, xn_ref, nw_ref)
    tm = x_ref.shape[0]
    proj = lambda rows, lo: jnp.dot(rows, w_ref[:, lo:lo + cdim], preferred_element_type=F32)
    b_gate = proj(h[:tm], 0)
    u, u_prev, u_next = _split_halo(proj(h, cdim) * proj(h, 2 * cdim), has_prev, has_next)
    u_dn, u_up = _shift_rows(u, u_prev, u_next)
    yc_ref[...] = b_gate * (cw_ref[0:1, :] * u_dn + cw_ref[1:2, :] * u + cw_ref[2:3, :] * u_up)


def _conv_mix(x, norm_w, w_c, conv_w, *, seq_len, tm):
    n, d = x.shape
    cdim = conv_w.shape[1]
    return pl.pallas_call(
        functools.partial(_conv_kernel, tiles_per_seq=seq_len // tm, cdim=cdim),
        grid=(n // tm,),
        in_specs=_halo_specs(tm, d, n) + [_resident(norm_w), _resident(w_c), _resident(conv_w)],
        out_specs=pl.BlockSpec((tm, cdim), lambda i: (i, 0)),
        out_shape=jax.ShapeDtypeStruct((n, cdim), F32),
        compiler_params=_params(("parallel",)),
        name="conv_mix",
    )(x, x, x, norm_w, w_c, conv_w)


def _rwkv_prep_kernel(x_ref, xp_ref, xn_ref, nw_ref, w_ref, mu_ref, w0_ref, a0_ref, kk_ref, ka_ref, rk_ref,
                      w2f_ref, w2b_ref, a2f_ref, a2b_ref, g2_ref,
                      r_ref, v_ref, kkn_ref, kkaf_ref, kkab_ref, kf_ref, kb_ref,
                      ldf_ref, ldb_ref, bonus_ref, g_ref, *, tiles_per_seq, rdim):
    has_prev, has_next = _halo_flags(tiles_per_seq)
    h = _normed_with_halo(x_ref, xp_ref, xn_ref, nw_ref)

    def shifted(lo, hi):
        z = jnp.dot(h, w_ref[:, lo:hi], preferred_element_type=F32)
        zc, z_prev, z_next = _split_halo(z, has_prev, has_next)
        dn, up = _shift_rows(zc, z_prev, z_next)
        return zc + mu_ref[:, lo:hi] * (0.5 * (dn + up) - zc)

    zl = shifted(3 * rdim, 3 * rdim + LORA_COLS)
    zg = shifted(3 * rdim + LORA_COLS, 3 * rdim + LORA_COLS + GATE_PAD)
    g_ref[...] = _mm(_sigmoid(zg), g2_ref[...])
    th = jnp.tanh(zl[:, 0:2 * DECAY_LORA])
    za = zl[:, 2 * DECAY_LORA:LORA_COLS]
    a_dirs = []
    for d, (w2_ref, a2_ref, ld_ref) in enumerate(((w2f_ref, a2f_ref, ldf_ref), (w2b_ref, a2b_ref, ldb_ref))):
        y = w0_ref[d:d + 1, :] + _mm(th, w2_ref[...])
        ld_ref[...] = -DECAY_SCALE * _sigmoid(y)
        a_dirs.append(_sigmoid(a0_ref[d:d + 1, :] + _mm(za, a2_ref[...])))

    k = shifted(rdim, 2 * rdim)
    kk = k * kk_ref[...]
    nrm = jnp.sqrt(_head_sum(kk * kk))
    kkn = kk / jnp.maximum(nrm, L2_EPS)
    kkn_ref[...] = kkn
    r = shifted(0, rdim)
    r_ref[...] = r
    rk_sum = jnp.zeros_like(r)
    for a, kka_ref, kd_ref in zip(a_dirs, (kkaf_ref, kkab_ref), (kf_ref, kb_ref)):
        k_d = k * (1.0 + (a - 1.0) * ka_ref[...])
        kka_ref[...] = kkn * a
        kd_ref[...] = k_d
        rk_sum = rk_sum + r * k_d * rk_ref[...]
    v = shifted(2 * rdim, 3 * rdim)
    v_ref[...] = v
    bonus_ref[...] = _head_sum(rk_sum) * v


def _rwkv_prep(x, norm_w, w_r, mu, w0, a0, k_k, k_a, r_k, w2f, w2b, a2f, a2b, g2p, *, seq_len, tm):
    n, d = x.shape
    rdim = k_k.shape[1]
    consts = (norm_w, w_r, mu, w0, a0, k_k, k_a, r_k, w2f, w2b, a2f, a2b, g2p)
    return pl.pallas_call(
        functools.partial(_rwkv_prep_kernel, tiles_per_seq=seq_len // tm, rdim=rdim),
        grid=(n // tm,),
        in_specs=_halo_specs(tm, d, n) + [_resident(a) for a in consts],
        out_specs=[pl.BlockSpec((tm, rdim), lambda i: (i, 0))] * 11,
        out_shape=[jax.ShapeDtypeStruct((n, rdim), F32)] * 11,
        compiler_params=_params(("parallel",)),
        name="rwkv_prep",
    )(x, x, x, *consts)


def _block_diag(y, head0):
    return jnp.concatenate([jnp.where(head0, y, 0.0), jnp.where(head0, 0.0, y)], axis=0)


def _unit_triangular_inverse(a_list):
    c = CHUNK
    rng = range(len(a_list))
    row = lax.broadcasted_iota(jnp.int32, (c, LANES), 0)
    lane = lax.broadcasted_iota(jnp.int32, (c, LANES), 1)
    head0 = lane < HEAD_SIZE
    eye = (lane % HEAD_SIZE == row).astype(F32)
    bd = lambda y: _block_diag(y, head0)
    t = [eye + a for a in a_list]
    p = [_mm(a, bd(a)) for a in a_list]
    for _ in range(c.bit_length() - 3):
        both = [_mm(jnp.concatenate([t[i], p[i]], axis=0), bd(p[i])) for i in rng]
        t = [t[i] + both[i][:c] for i in rng]
        p = [both[i][c:] for i in rng]
    return [t[i] + _mm(t[i], bd(p[i])) for i in rng]


def _scan_intra(ops, masks):
    head0, strict, incl, bd_mask, diag_mask = masks
    c = CHUNK
    rng = range(len(ops))
    at, bt, kt, rt, bh, kh, v, decay_c = (list(t) for t in zip(*ops))
    bd = lambda y: _block_diag(y, head0)

    a_all = [lax.dot_general(
        jnp.concatenate([at[i], rt[i]], axis=0).astype(BF16),
        jnp.concatenate([bd(bt[i]), bd(kt[i])], axis=0).astype(BF16),
        (((1,), (1,)), ((), ())), preferred_element_type=F32) for i in rng]
    a_ab = [jnp.where(strict, a_all[i][:c, :LANES], 0.0) for i in rng]
    a_ak = [jnp.where(strict, a_all[i][:c, LANES:], 0.0) for i in rng]
    a_rb = [jnp.where(incl, a_all[i][c:, :LANES], 0.0) for i in rng]
    a_rk = [jnp.where(incl, a_all[i][c:, LANES:], 0.0) for i in rng]

    t_inv = _unit_triangular_inverse(a_ab)

    xo = [_mm(jnp.concatenate([a_ak[i], a_rk[i]], axis=0), bd(v[i])) for i in rng]
    ap_w = [_mm(t_inv[i], jnp.concatenate([bd(at[i]), bd(xo[i][:c])], axis=1)) for i in rng]
    q_o = [_mm(a_rb[i], jnp.concatenate([bd(ap_w[i][:, :LANES]), bd(ap_w[i][:, LANES:])], axis=1)) for i in rng]
    lt = [jnp.transpose(jnp.concatenate([bh[i], kh[i]], axis=0)) for i in rng]
    zero = jnp.zeros((c, LANES), F32)
    mn = [_mm(lt[i], jnp.concatenate([ap_w[i], jnp.concatenate([zero, v[i]], axis=1)], axis=0))
          for i in rng]

    res = []
    for i in rng:
        qp = rt[i] + q_o[i][:, :LANES]
        m_bd = jnp.where(bd_mask, mn[i][:, :LANES], 0.0) + jnp.where(diag_mask, decay_c[i], 0.0)
        n_bd = jnp.where(bd_mask, mn[i][:, LANES:], 0.0)
        res.append((jnp.concatenate([qp, m_bd], axis=0), q_o[i][:, LANES:] + xo[i][c:], n_bd))
    return res


def _scan_kernel(r_ref, v_ref, kk_ref, kka_ref, kd_ref, ld_ref, o_ref, h_scr, *, reverse, pairs, chunks):
    c = CHUNK

    @pl.when(pl.program_id(1) == 0)
    def _():
        h_scr[...] = jnp.zeros_like(h_scr)

    t_i = lax.broadcasted_iota(jnp.int32, (c, c), 0)
    s_i = lax.broadcasted_iota(jnp.int32, (c, c), 1)
    tri = ((s_i >= t_i) if reverse else (s_i <= t_i)).astype(BF16)
    tp = lax.broadcasted_iota(jnp.int32, (c, LANES), 0)
    lane = lax.broadcasted_iota(jnp.int32, (c, LANES), 1)
    sp = lane % HEAD_SIZE
    head0 = lane < HEAD_SIZE
    strict = (sp > tp) if reverse else (sp < tp)
    incl = (sp >= tp) if reverse else (sp <= tp)
    r128 = lax.broadcasted_iota(jnp.int32, (LANES, LANES), 0)
    c128 = lax.broadcasted_iota(jnp.int32, (LANES, LANES), 1)
    bd_mask = (r128 < HEAD_SIZE) == (c128 < HEAD_SIZE)
    diag_mask = r128 == c128
    masks = (head0, strict, incl, bd_mask, diag_mask)
    last = 0 if reverse else c - 1

    group = SCAN_GROUP if chunks % SCAN_GROUP == 0 else 1

    def chunk_operands(rows):
        ld = ld_ref[rows, :]
        cum = sum(jnp.dot(tri, t, preferred_element_type=F32) for t in _split3(ld))
        cum_c = cum[last:last + 1, :]
        e_neg = jnp.exp(-cum)
        e_tail = jnp.exp(cum_c - cum)
        kka = kka_ref[rows, :]
        kd = kd_ref[rows, :]
        full = (-kk_ref[rows, :] * jnp.exp(cum - ld), kka * e_neg, kd * e_neg, r_ref[rows, :] * jnp.exp(cum),
                kka * e_tail, kd * e_tail, v_ref[rows, :], jnp.exp(cum_c))
        return [tuple(t[:, p * LANES:(p + 1) * LANES] for t in full) for p in range(pairs)]

    def group_body(gi, carry):
        rows = []
        for j in range(group):
            ci = gi * group + j
            cc = (chunks - 1 - ci) if reverse else ci
            rows.append(pl.ds(pl.multiple_of(cc * c, c), c))
        ops = [op for rw in rows for op in chunk_operands(rw)]
        intra = _scan_intra(ops, masks)
        h = [h_scr[p] for p in range(pairs)]
        for j in range(group):
            for p in range(pairs):
                lhs, o_intra, n_bd = intra[j * pairs + p]
                oh = _mm(lhs, h[p])
                o_ref[rows[j], p * LANES:(p + 1) * LANES] = oh[:c] + o_intra
                h[p] = oh[c:] + n_bd
        for p in range(pairs):
            h_scr[p] = h[p]
        return carry

    lax.fori_loop(0, chunks // group, group_body, 0)


def _scan(r, v, kk, kka, kd, ld, *, reverse, seq_len, tb):
    n, rdim = r.shape
    n_seq = n // seq_len
    bps = seq_len // tb
    pairs = rdim // LANES

    def idx(s, j):
        jj = (bps - 1 - j) if reverse else j
        return (s * bps + jj, 0)

    spec = pl.BlockSpec((tb, rdim), idx)
    return pl.pallas_call(
        functools.partial(_scan_kernel, reverse=reverse, pairs=pairs, chunks=tb // CHUNK),
        grid=(n_seq, bps),
        in_specs=[spec] * 6,
        out_specs=spec,
        out_shape=jax.ShapeDtypeStruct((n, rdim), F32),
        scratch_shapes=[pltpu.VMEM((pairs, LANES, LANES), F32)],
        compiler_params=_params(("parallel", "arbitrary")),
        name="scan_bwd" if reverse else "scan_fwd",
    )(r, v, kk, kka, kd, ld)


def _post_kernel(of_ref, ob_ref, bonus_ref, g_ref, yc_ref, x_ref, lnw_ref, lnb_ref, wc_ref, wr_ref, o_ref):
    o = of_ref[...] + ob_ref[...]
    mean = _head_sum(o) * (1.0 / HEAD_SIZE)
    d = o - mean
    var = _head_sum(d * d) * (1.0 / HEAD_SIZE)
    y = d * lax.rsqrt(var + GN_EPS) * lnw_ref[...] + lnb_ref[...] + bonus_ref[...]
    y = y * g_ref[...]
    o_ref[...] = x_ref[...] + _mm(yc_ref[...], wc_ref[...]) + _mm(y, wr_ref[...])


def _post(o_f, o_b, bonus, g, yc, x, ln_w, ln_b, w_conv, w_rwkv, *, tm):
    n, d = x.shape
    cdim = yc.shape[1]
    rdim = o_f.shape[1]
    rspec = pl.BlockSpec((tm, rdim), lambda i: (i, 0))
    full = lambda a: pl.BlockSpec(a.shape, lambda i: (0,) * a.ndim)
    return pl.pallas_call(
        _post_kernel,
        grid=(n // tm,),
        in_specs=[rspec, rspec, rspec, rspec, pl.BlockSpec((tm, cdim), lambda i: (i, 0)),
                  pl.BlockSpec((tm, d), lambda i: (i, 0)), full(ln_w), full(ln_b), full(w_conv), full(w_rwkv)],
        out_specs=pl.BlockSpec((tm, d), lambda i: (i, 0)),
        out_shape=jax.ShapeDtypeStruct((n, d), F32),
        compiler_params=_params(("parallel",)),
        name="post",
    )(o_f, o_b, bonus, g, yc, x, ln_w, ln_b, w_conv, w_rwkv)


def _tile(n, pref):
    t = min(pref, n)
    assert n % t == 0, (n, t)
    return t


def _pad_rows(w, rows_before, rows_total):
    return jnp.pad(w, ((rows_before, rows_total - rows_before - w.shape[0]), (0, 0)))


def _layer(x, seq_len, p):
    n, d = x.shape
    cdim = p["conv_w"].shape[1]
    rdim = p["k_k"].shape[1]
    tm = _tile(seq_len, 512)
    tm_ffn = _tile(seq_len, 1024)
    dff = p["ffn1_wg"].shape[1]
    tf = _tile(dff, 512)
    x1 = _ffn(x, p["ffn1_norm"], p["ffn1_wg"], p["ffn1_wu"], p["ffn1_wd"], p["final_norm"],
              final=False, tm=tm_ffn, tf=tf)
    tp = _tile(seq_len, 256)
    yc = _conv_mix(x1, p["mix_norm"], p["w_in_c"], p["conv_w"], seq_len=seq_len, tm=tm)
    (r, v, kk, kka_f, kka_b, k_f, k_b, ld_f, ld_b, bonus, g) = _rwkv_prep(
        x1, p["mix_norm"], p["w_in_r"], p["mu"], p["w0"], p["a0"], p["k_k"], p["k_a"], p["r_k"],
        p["w2f"], p["w2b"], p["a2f"], p["a2b"], p["g2"], seq_len=seq_len, tm=tp)
    tb = _tile(seq_len, 512)
    o_f = _scan(r, v, kk, kka_f, k_f, ld_f, reverse=False, seq_len=seq_len, tb=tb)
    o_b = _scan(r, v, kk, kka_b, k_b, ld_b, reverse=True, seq_len=seq_len, tb=tb)
    x2 = _post(o_f, o_b, bonus, g, yc, x1, p["ln_w"], p["ln_b"], p["w_out_c"], p["w_out_r"], tm=tp)
    return _ffn(x2, p["ffn2_norm"], p["ffn2_wg"], p["ffn2_wu"], p["ffn2_wd"], p["final_norm"],
                final=True, tm=tm_ffn, tf=tf)


def kernel(x_prompt, x_sample, ffn1_norm, ffn1_w_gate, ffn1_w_up, ffn1_w_down, mix_norm, w_in, conv_w, mu_shift,
           w0, w2, a0, a2, g2, k_k, k_a, r_k, ln_x_w, ln_x_b, w_out, ffn2_norm, ffn2_w_gate, ffn2_w_up,
           ffn2_w_down, final_norm):
    assert ffn1_norm.shape[0] == 1, "single-layer trunk"
    d = x_prompt.shape[-1]
    cdim = conv_w.shape[-1]
    rdim = k_k.shape[-1]
    in_cols = w_in.shape[-1]
    rwkv_cols = in_cols - 3 * cdim
    assert rwkv_cols == 3 * rdim + LORA_COLS + GATE_LORA
    pad_cols = 3 * rdim + LORA_COLS + GATE_PAD - rwkv_cols
    row = lambda a: a.reshape(1, -1).astype(F32)
    bf = lambda a: a.astype(BF16)
    p = {
        "ffn1_norm": row(ffn1_norm[0]), "ffn1_wg": bf(ffn1_w_gate[0]), "ffn1_wu": bf(ffn1_w_up[0]),
        "ffn1_wd": bf(ffn1_w_down[0]),
        "ffn2_norm": row(ffn2_norm[0]), "ffn2_wg": bf(ffn2_w_gate[0]), "ffn2_wu": bf(ffn2_w_up[0]),
        "ffn2_wd": bf(ffn2_w_down[0]),
        "final_norm": row(final_norm), "mix_norm": row(mix_norm[0]),
        "w_in_c": bf(w_in[0, :, :3 * cdim]),
        "w_in_r": bf(jnp.pad(w_in[0, :, 3 * cdim:], ((0, 0), (0, pad_cols)))),
        "mu": jnp.pad(mu_shift[0], (0, pad_cols)).reshape(1, -1),
        "conv_w": conv_w[0], "w0": w0[0], "a0": a0[0],
        "k_k": row(k_k[0]), "k_a": row(k_a[0]), "r_k": row(r_k[0]),
        "w2f": bf(_pad_rows(w2[0, 0], 0, 2 * DECAY_LORA)), "w2b": bf(_pad_rows(w2[0, 1], DECAY_LORA, 2 * DECAY_LORA)),
        "a2f": bf(_pad_rows(a2[0, 0], 0, 2 * AAA_LORA)), "a2b": bf(_pad_rows(a2[0, 1], AAA_LORA, 2 * AAA_LORA)),
        "g2": bf(_pad_rows(g2[0], 0, GATE_PAD)),
        "ln_w": row(ln_x_w[0]), "ln_b": row(ln_x_b[0]),
        "w_out_c": bf(w_out[0, :cdim]), "w_out_r": bf(w_out[0, cdim:]),
    }
    outs = []
    for x in (x_prompt, x_sample):
        b, t, _ = x.shape
        outs.append(_layer(x.reshape(b * t, d), t, p).reshape(b, t, d))
    return tuple(outs)
```

```python
import functools

import jax
import jax.numpy as jnp
from jax import lax
from jax.experimental import pallas as pl
from jax.experimental.pallas import tpu as pltpu

F32 = jnp.float32
BF16 = jnp.bfloat16

HEAD_SIZE = 64
LANES = 128
MXU_WIDTH = 256
SUBLANES = 8
CHUNK = 64
FFN_SLAB = 128
FFN_ROWS = 1024
SCAN_GROUP = 2
DECAY_LORA = 64
AAA_LORA = 64
GATE_LORA = 160
LORA_COLS = 2 * DECAY_LORA + 2 * AAA_LORA
GATE_PAD = 256
FFN_RESIDUAL_SCALE = 0.5
RMS_EPS = 1e-6
GN_EPS = 64e-5
L2_EPS = 1e-12
DECAY_SCALE = 0.6065306597126334
VMEM_LIMIT = 63 * 1024 * 1024


def _params(sem):
    return pltpu.CompilerParams(dimension_semantics=sem, vmem_limit_bytes=VMEM_LIMIT)


def _rms(x, w):
    return x * lax.rsqrt(jnp.mean(x * x, axis=-1, keepdims=True) + RMS_EPS) * w


def _sigmoid(x):
    return 1.0 / (1.0 + jnp.exp(-x))


def _mm(a, b):
    return jnp.dot(a.astype(BF16), b.astype(BF16), preferred_element_type=F32)


def _ffn_kernel(x_ref, nw_ref, wg_ref, wu_ref, wd_ref, fw_ref, o_ref, h_scr, *, final):
    j = pl.program_id(1)

    slabs = [pl.ds(r, FFN_SLAB) for r in range(0, x_ref.shape[0], FFN_SLAB)]

    @pl.when(j == 0)
    def _():
        for rows in slabs:
            h_scr[rows, :] = _rms(x_ref[rows, :], nw_ref[...]).astype(BF16)
        o_ref[...] = jnp.zeros_like(o_ref)

    half = min(FFN_ROWS, x_ref.shape[0])
    for r in range(0, x_ref.shape[0], half):
        rows = pl.ds(r, half)
        h = h_scr[rows, :]
        g = jnp.dot(h, wg_ref[...], preferred_element_type=F32)
        u = jnp.dot(h, wu_ref[...], preferred_element_type=F32)
        a = ((g * _sigmoid(g)) * u).astype(BF16)
        o_ref[rows, :] += jnp.dot(a, wd_ref[...], preferred_element_type=F32)

    @pl.when(j == pl.num_programs(1) - 1)
    def _():
        for rows in slabs:
            y = x_ref[rows, :] + FFN_RESIDUAL_SCALE * o_ref[rows, :]
            if final:
                y = _rms(y, fw_ref[...])
            o_ref[rows, :] = y


def _ffn(x, norm_w, wg, wu, wd, final_w, *, final, tm, tf):
    n, d = x.shape
    dff = wg.shape[1]
    return pl.pallas_call(
        functools.partial(_ffn_kernel, final=final),
        grid=(n // tm, dff // tf),
        in_specs=[
            pl.BlockSpec((tm, d), lambda i, j: (i, 0)),
            pl.BlockSpec((1, d), lambda i, j: (0, 0)),
            pl.BlockSpec((d, tf), lambda i, j: (0, j)),
            pl.BlockSpec((d, tf), lambda i, j: (0, j)),
            pl.BlockSpec((tf, d), lambda i, j: (j, 0)),
            pl.BlockSpec((1, d), lambda i, j: (0, 0)),
        ],
        out_specs=pl.BlockSpec((tm, d), lambda i, j: (i, 0)),
        out_shape=jax.ShapeDtypeStruct((n, d), F32),
        scratch_shapes=[pltpu.VMEM((tm, d), BF16)],
        compiler_params=_params(("parallel", "arbitrary")),
        name="ffn_final" if final else "ffn",
    )(x, norm_w, wg, wu, wd, final_w)


def _split3(x):
    hi = x.astype(BF16)
    rest = x - hi.astype(F32)
    mid = rest.astype(BF16)
    lo = (rest - mid.astype(F32)).astype(BF16)
    return hi, mid, lo


def _head_ones(width):
    r = lax.broadcasted_iota(jnp.int32, (width, width), 0) // HEAD_SIZE
    c = lax.broadcasted_iota(jnp.int32, (width, width), 1) // HEAD_SIZE
    return (r == c).astype(BF16)


def _head_sum(x):
    width = MXU_WIDTH if x.shape[1] % MXU_WIDTH == 0 else LANES
    ones_bd = _head_ones(width)
    parts = []
    for lo in range(0, x.shape[1], width):
        hi_mid_lo = _split3(x[:, lo:lo + width])
        parts.append(sum(jnp.dot(t, ones_bd, preferred_element_type=F32) for t in hi_mid_lo))
    return parts[0] if len(parts) == 1 else jnp.concatenate(parts, axis=1)


def _shift_rows(x, prev_row, next_row):
    rows = x.shape[0]
    ridx = lax.broadcasted_iota(jnp.int32, x.shape, 0)
    dn = jnp.where(ridx == 0, prev_row, pltpu.roll(x, 1, axis=0))
    up = jnp.where(ridx == rows - 1, next_row, pltpu.roll(x, rows - 1, axis=0))
    return dn, up


def _halo_specs(tm, d, n):
    sub_per_tile = tm // SUBLANES
    last_sub = n // SUBLANES - 1
    return [
        pl.BlockSpec((tm, d), lambda i: (i, 0)),
        pl.BlockSpec((SUBLANES, d), lambda i: (jnp.maximum(i * sub_per_tile - 1, 0), 0)),
        pl.BlockSpec((SUBLANES, d), lambda i: (jnp.minimum((i + 1) * sub_per_tile, last_sub), 0)),
    ]


def _resident(a):
    return pl.BlockSpec(a.shape, lambda i: (0,) * a.ndim, pipeline_mode=pl.Buffered(1))


def _normed_with_halo(x_ref, xp_ref, xn_ref, nw_ref):
    x_all = jnp.concatenate([x_ref[...], xp_ref[...], xn_ref[...]], axis=0)
    return _rms(x_all, nw_ref[...]).astype(BF16)


def _halo_flags(tiles_per_seq):
    i = pl.program_id(0)
    has_prev = (i % tiles_per_seq != 0).astype(F32)
    has_next = (i % tiles_per_seq != tiles_per_seq - 1).astype(F32)
    return has_prev, has_next


def _split_halo(z, has_prev, has_next):
    tm = z.shape[0] - 2 * SUBLANES
    prev_at = tm + SUBLANES - 1
    next_at = tm + SUBLANES
    return z[:tm], z[prev_at:prev_at + 1] * has_prev, z[next_at:next_at + 1] * has_next


def _conv_kernel(x_ref, xp_ref, xn_ref, nw_ref, w_ref, cw_ref, yc_ref, *, tiles_per_seq, cdim):
    has_prev, has_next = _halo_flags(tiles_per_seq)
    h = _normed_with_halo(x_ref, xp_ref, xn_ref, nw_ref)
    tm = x_ref.shape[0]
    proj = lambda rows, lo: jnp.dot(rows, w_ref[:, lo:lo + cdim], preferred_element_type=F32)
    b_gate = proj(h[:tm], 0)
    u, u_prev, u_next = _split_halo(proj(h, cdim) * proj(h, 2 * cdim), has_prev, has_next)
    u_dn, u_up = _shift_rows(u, u_prev, u_next)
    yc_ref[...] = b_gate * (cw_ref[0:1, :] * u_dn + cw_ref[1:2, :] * u + cw_ref[2:3, :] * u_up)


def _conv_mix(x, norm_w, w_c, conv_w, *, seq_len, tm):
    n, d = x.shape
    cdim = conv_w.shape[1]
    return pl.pallas_call(
        functools.partial(_conv_kernel, tiles_per_seq=seq_len // tm, cdim=cdim),
        grid=(n // tm,),
        in_specs=_halo_specs(tm, d, n) + [_resident(norm_w), _resident(w_c), _resident(conv_w)],
        out_specs=pl.BlockSpec((tm, cdim), lambda i: (i, 0)),
        out_shape=jax.ShapeDtypeStruct((n, cdim), F32),
        compiler_params=_params(("parallel",)),
        name="conv_mix",
    )(x, x, x, norm_w, w_c, conv_w)


def _rwkv_prep_kernel(x_ref, xp_ref, xn_ref, nw_ref, w_ref, mu_ref, w0_ref, a0_ref, kk_ref, ka_ref, rk_ref,
                      w2f_ref, w2b_ref, a2f_ref, a2b_ref, g2_ref,
                      r_ref, v_ref, kkn_ref, kkaf_ref, kkab_ref, kf_ref, kb_ref,
                      ldf_ref, ldb_ref, bonus_ref, g_ref, *, tiles_per_seq, rdim):
    has_prev, has_next = _halo_flags(tiles_per_seq)
    h = _normed_with_halo(x_ref, xp_ref, xn_ref, nw_ref)

    def shifted(lo, hi):
        z = jnp.dot(h, w_ref[:, lo:hi], preferred_element_type=F32)
        zc, z_prev, z_next = _split_halo(z, has_prev, has_next)
        dn, up = _shift_rows(zc, z_prev, z_next)
        return zc + mu_ref[:, lo:hi] * (0.5 * (dn + up) - zc)

    zl = shifted(3 * rdim, 3 * rdim + LORA_COLS)
    zg = shifted(3 * rdim + LORA_COLS, 3 * rdim + LORA_COLS + GATE_PAD)
    g_ref[...] = _mm(_sigmoid(zg), g2_ref[...])
    th = jnp.tanh(zl[:, 0:2 * DECAY_LORA])
    za = zl[:, 2 * DECAY_LORA:LORA_COLS]
    a_dirs = []
    for d, (w2_ref, a2_ref, ld_ref) in enumerate(((w2f_ref, a2f_ref, ldf_ref), (w2b_ref, a2b_ref, ldb_ref))):
        y = w0_ref[d:d + 1, :] + _mm(th, w2_ref[...])
        ld_ref[...] = -DECAY_SCALE * _sigmoid(y)
        a_dirs.append(_sigmoid(a0_ref[d:d + 1, :] + _mm(za, a2_ref[...])))

    k = shifted(rdim, 2 * rdim)
    kk = k * kk_ref[...]
    nrm = jnp.sqrt(_head_sum(kk * kk))
    kkn = kk / jnp.maximum(nrm, L2_EPS)
    kkn_ref[...] = kkn
    r = shifted(0, rdim)
    r_ref[...] = r
    rk_sum = jnp.zeros_like(r)
    for a, kka_ref, kd_ref in zip(a_dirs, (kkaf_ref, kkab_ref), (kf_ref, kb_ref)):
        k_d = k * (1.0 + (a - 1.0) * ka_ref[...])
        kka_ref[...] = kkn * a
        kd_ref[...] = k_d
        rk_sum = rk_sum + r * k_d * rk_ref[...]
    v = shifted(2 * rdim, 3 * rdim)
    v_ref[...] = v
    bonus_ref[...] = _head_sum(rk_sum) * v


def _rwkv_prep(x, norm_w, w_r, mu, w0, a0, k_k, k_a, r_k, w2f, w2b, a2f, a2b, g2p, *, seq_len, tm):
    n, d = x.shape
    rdim = k_k.shape[1]
    consts = (norm_w, w_r, mu, w0, a0, k_k, k_a, r_k, w2f, w2b, a2f, a2b, g2p)
    return pl.pallas_call(
        functools.partial(_rwkv_prep_kernel, tiles_per_seq=seq_len // tm, rdim=rdim),
        grid=(n // tm,),
        in_specs=_halo_specs(tm, d, n) + [_resident(a) for a in consts],
        out_specs=[pl.BlockSpec((tm, rdim), lambda i: (i, 0))] * 11,
        out_shape=[jax.ShapeDtypeStruct((n, rdim), F32)] * 11,
        compiler_params=_params(("parallel",)),
        name="rwkv_prep",
    )(x, x, x, *consts)


def _block_diag(y, head0):
    return jnp.concatenate([jnp.where(head0, y, 0.0), jnp.where(head0, 0.0, y)], axis=0)


def _unit_triangular_inverse(a_list):
    c = CHUNK
    rng = range(len(a_list))
    row = lax.broadcasted_iota(jnp.int32, (c, LANES), 0)
    lane = lax.broadcasted_iota(jnp.int32, (c, LANES), 1)
    head0 = lane < HEAD_SIZE
    eye = (lane % HEAD_SIZE == row).astype(F32)
    bd = lambda y: _block_diag(y, head0)
    t = [eye + a for a in a_list]
    p = [_mm(a, bd(a)) for a in a_list]
    for _ in range(c.bit_length() - 3):
        both = [_mm(jnp.concatenate([t[i], p[i]], axis=0), bd(p[i])) for i in rng]
        t = [t[i] + both[i][:c] for i in rng]
        p = [both[i][c:] for i in rng]
    return [t[i] + _mm(t[i], bd(p[i])) for i in rng]


def _scan_intra(ops, masks):
    head0, strict, incl, bd_mask, diag_mask = masks
    c = CHUNK
    rng = range(len(ops))
    at, bt, kt, rt, bh, kh, v, decay_c = (list(t) for t in zip(*ops))
    bd = lambda y: _block_diag(y, head0)

    a_all = [lax.dot_general(
        jnp.concatenate([at[i], rt[i]], axis=0).astype(BF16),
        jnp.concatenate([bd(bt[i]), bd(kt[i])], axis=0).astype(BF16),
        (((1,), (1,)), ((), ())), preferred_element_type=F32) for i in rng]
    a_ab = [jnp.where(strict, a_all[i][:c, :LANES], 0.0) for i in rng]
    a_ak = [jnp.where(strict, a_all[i][:c, LANES:], 0.0) for i in rng]
    a_rb = [jnp.where(incl, a_all[i][c:, :LANES], 0.0) for i in rng]
    a_rk = [jnp.where(incl, a_all[i][c:, LANES:], 0.0) for i in rng]

    t_inv = _unit_triangular_inverse(a_ab)

    xo = [_mm(jnp.concatenate([a_ak[i], a_rk[i]], axis=0), bd(v[i])) for i in rng]
    ap_w = [_mm(t_inv[i], jnp.concatenate([bd(at[i]), bd(xo[i][:c])], axis=1)) for i in rng]
    q_o = [_mm(a_rb[i], jnp.concatenate([bd(ap_w[i][:, :LANES]), bd(ap_w[i][:, LANES:])], axis=1)) for i in rng]
    lt = [jnp.transpose(jnp.concatenate([bh[i], kh[i]], axis=0)) for i in rng]
    zero = jnp.zeros((c, LANES), F32)
    mn = [_mm(lt[i], jnp.concatenate([ap_w[i], jnp.concatenate([zero, v[i]], axis=1)], axis=0))
          for i in rng]

    res = []
    for i in rng:
        qp = rt[i] + q_o[i][:, :LANES]
        m_bd = jnp.where(bd_mask, mn[i][:, :LANES], 0.0) + jnp.where(diag_mask, decay_c[i], 0.0)
        n_bd = jnp.where(bd_mask, mn[i][:, LANES:], 0.0)
        res.append((jnp.concatenate([qp, m_bd], axis=0), q_o[i][:, LANES:] + xo[i][c:], n_bd))
    return res


def _scan_kernel(r_ref, v_ref, kk_ref, kka_ref, kd_ref, ld_ref, o_ref, h_scr, *, reverse, pairs, chunks):
    c = CHUNK

    @pl.when(pl.program_id(1) == 0)
    def _():
        h_scr[...] = jnp.zeros_like(h_scr)

    t_i = lax.broadcasted_iota(jnp.int32, (c, c), 0)
    s_i = lax.broadcasted_iota(jnp.int32, (c, c), 1)
    tri = ((s_i >= t_i) if reverse else (s_i <= t_i)).astype(BF16)
    tp = lax.broadcasted_iota(jnp.int32, (c, LANES), 0)
    lane = lax.broadcasted_iota(jnp.int32, (c, LANES), 1)
    sp = lane % HEAD_SIZE
    head0 = lane < HEAD_SIZE
    strict = (sp > tp) if reverse else (sp < tp)
    incl = (sp >= tp) if reverse else (sp <= tp)
    r128 = lax.broadcasted_iota(jnp.int32, (LANES, LANES), 0)
    c128 = lax.broadcasted_iota(jnp.int32, (LANES, LANES), 1)
    bd_mask = (r128 < HEAD_SIZE) == (c128 < HEAD_SIZE)
    diag_mask = r128 == c128
    masks = (head0, strict, incl, bd_mask, diag_mask)
    last = 0 if reverse else c - 1

    group = SCAN_GROUP if chunks % SCAN_GROUP == 0 else 1

    def chunk_operands(rows):
        ld = ld_ref[rows, :]
        cum = sum(jnp.dot(tri, t, preferred_element_type=F32) for t in _split3(ld))
        cum_c = cum[last:last + 1, :]
        e_neg = jnp.exp(-cum)
        e_tail = jnp.exp(cum_c - cum)
        kka = kka_ref[rows, :]
        kd = kd_ref[rows, :]
        full = (-kk_ref[rows, :] * jnp.exp(cum - ld), kka * e_neg, kd * e_neg, r_ref[rows, :] * jnp.exp(cum),
                kka * e_tail, kd * e_tail, v_ref[rows, :], jnp.exp(cum_c))
        return [tuple(t[:, p * LANES:(p + 1) * LANES] for t in full) for p in range(pairs)]

    def group_body(gi, carry):
        rows = []
        for j in range(group):
            ci = gi * group + j
            cc = (chunks - 1 - ci) if reverse else ci
            rows.append(pl.ds(pl.multiple_of(cc * c, c), c))
        ops = [op for rw in rows for op in chunk_operands(rw)]
        intra = _scan_intra(ops, masks)
        h = [h_scr[p] for p in range(pairs)]
        for j in range(group):
            for p in range(pairs):
                lhs, o_intra, n_bd = intra[j * pairs + p]
                oh = _mm(lhs, h[p])
                o_ref[rows[j], p * LANES:(p + 1) * LANES] = oh[:c] + o_intra
                h[p] = oh[c:] + n_bd
        for p in range(pairs):
            h_scr[p] = h[p]
        return carry

    lax.fori_loop(0, chunks // group, group_body, 0)


def _scan(r, v, kk, kka, kd, ld, *, reverse, seq_len, tb):
    n, rdim = r.shape
    n_seq = n // seq_len
    bps = seq_len // tb
    pairs = rdim // LANES

    def idx(s, j):
        jj = (bps - 1 - j) if reverse else j
        return (s * bps + jj, 0)

    spec = pl.BlockSpec((tb, rdim), idx)
    return pl.pallas_call(
        functools.partial(_scan_kernel, reverse=reverse, pairs=pairs, chunks=tb // CHUNK),
        grid=(n_seq, bps),
        in_specs=[spec] * 6,
        out_specs=spec,
        out_shape=jax.ShapeDtypeStruct((n, rdim), F32),
        scratch_shapes=[pltpu.VMEM((pairs, LANES, LANES), F32)],
        compiler_params=_params(("parallel", "arbitrary")),
        name="scan_bwd" if reverse else "scan_fwd",
    )(r, v, kk, kka, kd, ld)


def _post_kernel(of_ref, ob_ref, bonus_ref, g_ref, yc_ref, x_ref, lnw_ref, lnb_ref, wc_ref, wr_ref, o_ref):
    o = of_ref[...] + ob_ref[...]
    mean = _head_sum(o) * (1.0 / HEAD_SIZE)
    d = o - mean
    var = _head_sum(d * d) * (1.0 / HEAD_SIZE)
    y = d * lax.rsqrt(var + GN_EPS) * lnw_ref[...] + lnb_ref[...] + bonus_ref[...]
    y = y * g_ref[...]
    o_ref[...] = x_ref[...] + _mm(yc_ref[...], wc_ref[...]) + _mm(y, wr_ref[...])


def _post(o_f, o_b, bonus, g, yc, x, ln_w, ln_b, w_conv, w_rwkv, *, tm):
    n, d = x.shape
    cdim = yc.shape[1]
    rdim = o_f.shape[1]
    rspec = pl.BlockSpec((tm, rdim), lambda i: (i, 0))
    full = lambda a: pl.BlockSpec(a.shape, lambda i: (0,) * a.ndim)
    return pl.pallas_call(
        _post_kernel,
        grid=(n // tm,),
        in_specs=[rspec, rspec, rspec, rspec, pl.BlockSpec((tm, cdim), lambda i: (i, 0)),
                  pl.BlockSpec((tm, d), lambda i: (i, 0)), full(ln_w), full(ln_b), full(w_conv), full(w_rwkv)],
        out_specs=pl.BlockSpec((tm, d), lambda i: (i, 0)),
        out_shape=jax.ShapeDtypeStruct((n, d), F32),
        compiler_params=_params(("parallel",)),
        name="post",
    )(o_f, o_b, bonus, g, yc, x, ln_w, ln_b, w_conv, w_rwkv)


def _tile(n, pref):
    t = min(pref, n)
    assert n % t == 0, (n, t)
    return t


def _pad_rows(w, rows_before, rows_total):
    return jnp.pad(w, ((rows_before, rows_total - rows_before - w.shape[0]), (0, 0)))


def _layer(x, seq_len, p):
    n, d = x.shape
    cdim = p["conv_w"].shape[1]
    rdim = p["k_k"].shape[1]
    tm = _tile(seq_len, 512)
    tm_ffn = _tile(seq_len, 1024)
    dff = p["ffn1_wg"].shape[1]
    tf = _tile(dff, 512)
    x1 = _ffn(x, p["ffn1_norm"], p["ffn1_wg"], p["ffn1_wu"], p["ffn1_wd"], p["final_norm"],
              final=False, tm=tm_ffn, tf=tf)
    tp = _tile(seq_len, 256)
    yc = _conv_mix(x1, p["mix_norm"], p["w_in_c"], p["conv_w"], seq_len=seq_len, tm=tm)
    (r, v, kk, kka_f, kka_b, k_f, k_b, ld_f, ld_b, bonus, g) = _rwkv_prep(
        x1, p["mix_norm"], p["w_in_r"], p["mu"], p["w0"], p["a0"], p["k_k"], p["k_a"], p["r_k"],
        p["w2f"], p["w2b"], p["a2f"], p["a2b"], p["g2"], seq_len=seq_len, tm=tp)
    tb = _tile(seq_len, 512)
    o_f = _scan(r, v, kk, kka_f, k_f, ld_f, reverse=False, seq_len=seq_len, tb=tb)
    o_b = _scan(r, v, kk, kka_b, k_b, ld_b, reverse=True, seq_len=seq_len, tb=tb)
    x2 = _post(o_f, o_b, bonus, g, yc, x1, p["ln_w"], p["ln_b"], p["w_out_c"], p["w_out_r"], tm=tp)
    return _ffn(x2, p["ffn2_norm"], p["ffn2_wg"], p["ffn2_wu"], p["ffn2_wd"], p["final_norm"],
                final=True, tm=tm_ffn, tf=tf)


def kernel(x_prompt, x_sample, ffn1_norm, ffn1_w_gate, ffn1_w_up, ffn1_w_down, mix_norm, w_in, conv_w, mu_shift,
           w0, w2, a0, a2, g2, k_k, k_a, r_k, ln_x_w, ln_x_b, w_out, ffn2_norm, ffn2_w_gate, ffn2_w_up,
           ffn2_w_down, final_norm):
    assert ffn1_norm.shape[0] == 1, "single-layer trunk"
    d = x_prompt.shape[-1]
    cdim = conv_w.shape[-1]
    rdim = k_k.shape[-1]
    in_cols = w_in.shape[-1]
    rwkv_cols = in_cols - 3 * cdim
    assert rwkv_cols == 3 * rdim + LORA_COLS + GATE_LORA
    pad_cols = 3 * rdim + LORA_COLS + GATE_PAD - rwkv_cols
    row = lambda a: a.reshape(1, -1).astype(F32)
    bf = lambda a: a.astype(BF16)
    p = {
        "ffn1_norm": row(ffn1_norm[0]), "ffn1_wg": bf(ffn1_w_gate[0]), "ffn1_wu": bf(ffn1_w_up[0]),
        "ffn1_wd": bf(ffn1_w_down[0]),
        "ffn2_norm": row(ffn2_norm[0]), "ffn2_wg": bf(ffn2_w_gate[0]), "ffn2_wu": bf(ffn2_w_up[0]),
        "ffn2_wd": bf(ffn2_w_down[0]),
        "final_norm": row(final_norm), "mix_norm": row(mix_norm[0]),
        "w_in_c": bf(w_in[0, :, :3 * cdim]),
        "w_in_r": bf(jnp.pad(w_in[0, :, 3 * cdim:], ((0, 0), (0, pad_cols)))),
        "mu": jnp.pad(mu_shift[0], (0, pad_cols)).reshape(1, -1),
        "conv_w": conv_w[0], "w0": w0[0], "a0": a0[0],
        "k_k": row(k_k[0]), "k_a": row(k_a[0]), "r_k": row(r_k[0]),
        "w2f": bf(_pad_rows(w2[0, 0], 0, 2 * DECAY_LORA)), "w2b": bf(_pad_rows(w2[0, 1], DECAY_LORA, 2 * DECAY_LORA)),
        "a2f": bf(_pad_rows(a2[0, 0], 0, 2 * AAA_LORA)), "a2b": bf(_pad_rows(a2[0, 1], AAA_LORA, 2 * AAA_LORA)),
        "g2": bf(_pad_rows(g2[0], 0, GATE_PAD)),
        "ln_w": row(ln_x_w[0]), "ln_b": row(ln_x_b[0]),
        "w_out_c": bf(w_out[0, :cdim]), "w_out_r": bf(w_out[0, cdim:]),
    }
    outs = []
    for x in (x_prompt, x_sample):
        b, t, _ = x.shape
        outs.append(_layer(x.reshape(b * t, d), t, p).reshape(b, t, d))
    return tuple(outs)
```

```python
import functools

import jax
import jax.numpy as jnp
from jax import lax
from jax.experimental import pallas as pl
from jax.experimental.pallas import tpu as pltpu

F32 = jnp.float32
BF16 = jnp.bfloat16

HEAD_SIZE = 64
LANES = 128
MXU_WIDTH = 256
SUBLANES = 8
CHUNK = 64
FFN_SLAB = 128
FFN_ROWS = 512
SCAN_GROUP = 2
DECAY_LORA = 64
AAA_LORA = 64
GATE_LORA = 160
LORA_COLS = 2 * DECAY_LORA + 2 * AAA_LORA
GATE_PAD = 256
FFN_RESIDUAL_SCALE = 0.5
RMS_EPS = 1e-6
GN_EPS = 64e-5
L2_EPS = 1e-12
DECAY_SCALE = 0.6065306597126334
VMEM_LIMIT = 60 * 1024 * 1024


def _params(sem):
    return pltpu.CompilerParams(dimension_semantics=sem, vmem_limit_bytes=VMEM_LIMIT)


def _rms(x, w):
    return x * lax.rsqrt(jnp.mean(x * x, axis=-1, keepdims=True) + RMS_EPS) * w


def _sigmoid(x):
    return 1.0 / (1.0 + jnp.exp(-x))


def _mm(a, b):
    return jnp.dot(a.astype(BF16), b.astype(BF16), preferred_element_type=F32)


def _ffn_kernel(x_ref, nw_ref, wg_ref, wu_ref, wd_ref, fw_ref, o_ref, h_scr, *, final):
    j = pl.program_id(1)

    slabs = [pl.ds(r, FFN_SLAB) for r in range(0, x_ref.shape[0], FFN_SLAB)]

    @pl.when(j == 0)
    def _():
        for rows in slabs:
            h_scr[rows, :] = _rms(x_ref[rows, :], nw_ref[...]).astype(BF16)
        o_ref[...] = jnp.zeros_like(o_ref)

    half = min(FFN_ROWS, x_ref.shape[0])
    for r in range(0, x_ref.shape[0], half):
        rows = pl.ds(r, half)
        h = h_scr[rows, :]
        g = jnp.dot(h, wg_ref[...], preferred_element_type=F32)
        u = jnp.dot(h, wu_ref[...], preferred_element_type=F32)
        a = ((g * _sigmoid(g)) * u).astype(BF16)
        o_ref[rows, :] += jnp.dot(a, wd_ref[...], preferred_element_type=F32)

    @pl.when(j == pl.num_programs(1) - 1)
    def _():
        for rows in slabs:
            y = x_ref[rows, :] + FFN_RESIDUAL_SCALE * o_ref[rows, :]
            if final:
                y = _rms(y, fw_ref[...])
            o_ref[rows, :] = y


def _ffn(x, norm_w, wg, wu, wd, final_w, *, final, tm, tf):
    n, d = x.shape
    dff = wg.shape[1]
    return pl.pallas_call(
        functools.partial(_ffn_kernel, final=final),
        grid=(n // tm, dff // tf),
        in_specs=[
            pl.BlockSpec((tm, d), lambda i, j: (i, 0)),
            pl.BlockSpec((1, d), lambda i, j: (0, 0)),
            pl.BlockSpec((d, tf), lambda i, j: (0, j)),
            pl.BlockSpec((d, tf), lambda i, j: (0, j)),
            pl.BlockSpec((tf, d), lambda i, j: (j, 0)),
            pl.BlockSpec((1, d), lambda i, j: (0, 0)),
        ],
        out_specs=pl.BlockSpec((tm, d), lambda i, j: (i, 0)),
        out_shape=jax.ShapeDtypeStruct((n, d), F32),
        scratch_shapes=[pltpu.VMEM((tm, d), BF16)],
        compiler_params=_params(("parallel", "arbitrary")),
        name="ffn_final" if final else "ffn",
    )(x, norm_w, wg, wu, wd, final_w)


def _split3(x):
    hi = x.astype(BF16)
    rest = x - hi.astype(F32)
    mid = rest.astype(BF16)
    lo = (rest - mid.astype(F32)).astype(BF16)
    return hi, mid, lo


def _head_ones(width):
    r = lax.broadcasted_iota(jnp.int32, (width, width), 0) // HEAD_SIZE
    c = lax.broadcasted_iota(jnp.int32, (width, width), 1) // HEAD_SIZE
    return (r == c).astype(BF16)


def _head_sum(x):
    width = MXU_WIDTH if x.shape[1] % MXU_WIDTH == 0 else LANES
    ones_bd = _head_ones(width)
    parts = []
    for lo in range(0, x.shape[1], width):
        hi_mid_lo = _split3(x[:, lo:lo + width])
        parts.append(sum(jnp.dot(t, ones_bd, preferred_element_type=F32) for t in hi_mid_lo))
    return parts[0] if len(parts) == 1 else jnp.concatenate(parts, axis=1)


def _shift_rows(x, prev_row, next_row):
    rows = x.shape[0]
    ridx = lax.broadcasted_iota(jnp.int32, x.shape, 0)
    dn = jnp.where(ridx == 0, prev_row, pltpu.roll(x, 1, axis=0))
    up = jnp.where(ridx == rows - 1, next_row, pltpu.roll(x, rows - 1, axis=0))
    return dn, up


def _halo_specs(tm, d, n):
    sub_per_tile = tm // SUBLANES
    last_sub = n // SUBLANES - 1
    return [
        pl.BlockSpec((tm, d), lambda i: (i, 0)),
        pl.BlockSpec((SUBLANES, d), lambda i: (jnp.maximum(i * sub_per_tile - 1, 0), 0)),
        pl.BlockSpec((SUBLANES, d), lambda i: (jnp.minimum((i + 1) * sub_per_tile, last_sub), 0)),
    ]


def _resident(a):
    return pl.BlockSpec(a.shape, lambda i: (0,) * a.ndim, pipeline_mode=pl.Buffered(1))


def _normed_with_halo(x_ref, xp_ref, xn_ref, nw_ref):
    x_all = jnp.concatenate([x_ref[...], xp_ref[...], xn_ref[...]], axis=0)
    return _rms(x_all, nw_ref[...]).astype(BF16)


def _halo_flags(tiles_per_seq):
    i = pl.program_id(0)
    has_prev = (i % tiles_per_seq != 0).astype(F32)
    has_next = (i % tiles_per_seq != tiles_per_seq - 1).astype(F32)
    return has_prev, has_next


def _split_halo(z, has_prev, has_next):
    tm = z.shape[0] - 2 * SUBLANES
    prev_at = tm + SUBLANES - 1
    next_at = tm + SUBLANES
    return z[:tm], z[prev_at:prev_at + 1] * has_prev, z[next_at:next_at + 1] * has_next


def _conv_kernel(x_ref, xp_ref, xn_ref, nw_ref, w_ref, cw_ref, yc_ref, *, tiles_per_seq, cdim):
    has_prev, has_next = _halo_flags(tiles_per_seq)
    h = _normed_with_halo(x_ref, xp_ref, xn_ref, nw_ref)
    tm = x_ref.shape[0]
    proj = lambda rows, lo: jnp.dot(rows, w_ref[:, lo:lo + cdim], preferred_element_type=F32)
    b_gate = proj(h[:tm], 0)
    u, u_prev, u_next = _split_halo(proj(h, cdim) * proj(h, 2 * cdim), has_prev, has_next)
    u_dn, u_up = _shift_rows(u, u_prev, u_next)
    yc_ref[...] = b_gate * (cw_ref[0:1, :] * u_dn + cw_ref[1:2, :] * u + cw_ref[2:3, :] * u_up)


def _conv_mix(x, norm_w, w_c, conv_w, *, seq_len, tm):
    n, d = x.shape
    cdim = conv_w.shape[1]
    return pl.pallas_call(
        functools.partial(_conv_kernel, tiles_per_seq=seq_len // tm, cdim=cdim),
        grid=(n // tm,),
        in_specs=_halo_specs(tm, d, n) + [_resident(norm_w), _resident(w_c), _resident(conv_w)],
        out_specs=pl.BlockSpec((tm, cdim), lambda i: (i, 0)),
        out_shape=jax.ShapeDtypeStruct((n, cdim), F32),
        compiler_params=_params(("parallel",)),
        name="conv_mix",
    )(x, x, x, norm_w, w_c, conv_w)


def _rwkv_prep_kernel(x_ref, xp_ref, xn_ref, nw_ref, w_ref, mu_ref, w0_ref, a0_ref, kk_ref, ka_ref, rk_ref,
                      w2f_ref, w2b_ref, a2f_ref, a2b_ref, g2_ref,
                      r_ref, v_ref, kkn_ref, kkaf_ref, kkab_ref, kf_ref, kb_ref,
                      ldf_ref, ldb_ref, bonus_ref, g_ref, *, tiles_per_seq, rdim):
    has_prev, has_next = _halo_flags(tiles_per_seq)
    h = _normed_with_halo(x_ref, xp_ref, xn_ref, nw_ref)

    def shifted(lo, hi):
        z = jnp.dot(h, w_ref[:, lo:hi], preferred_element_type=F32)
        zc, z_prev, z_next = _split_halo(z, has_prev, has_next)
        dn, up = _shift_rows(zc, z_prev, z_next)
        return zc + mu_ref[:, lo:hi] * (0.5 * (dn + up) - zc)

    zl = shifted(3 * rdim, 3 * rdim + LORA_COLS)
    zg = shifted(3 * rdim + LORA_COLS, 3 * rdim + LORA_COLS + GATE_PAD)
    g_ref[...] = _mm(_sigmoid(zg), g2_ref[...])
    th = jnp.tanh(zl[:, 0:2 * DECAY_LORA])
    za = zl[:, 2 * DECAY_LORA:LORA_COLS]
    a_dirs = []
    for d, (w2_ref, a2_ref, ld_ref) in enumerate(((w2f_ref, a2f_ref, ldf_ref), (w2b_ref, a2b_ref, ldb_ref))):
        y = w0_ref[d:d + 1, :] + _mm(th, w2_ref[...])
        ld_ref[...] = -DECAY_SCALE * _sigmoid(y)
        a_dirs.append(_sigmoid(a0_ref[d:d + 1, :] + _mm(za, a2_ref[...])))

    k = shifted(rdim, 2 * rdim)
    kk = k * kk_ref[...]
    nrm = jnp.sqrt(_head_sum(kk * kk))
    kkn = kk / jnp.maximum(nrm, L2_EPS)
    kkn_ref[...] = kkn
    r = shifted(0, rdim)
    r_ref[...] = r
    rk_sum = jnp.zeros_like(r)
    for a, kka_ref, kd_ref in zip(a_dirs, (kkaf_ref, kkab_ref), (kf_ref, kb_ref)):
        k_d = k * (1.0 + (a - 1.0) * ka_ref[...])
        kka_ref[...] = kkn * a
        kd_ref[...] = k_d
        rk_sum = rk_sum + r * k_d * rk_ref[...]
    v = shifted(2 * rdim, 3 * rdim)
    v_ref[...] = v
    bonus_ref[...] = _head_sum(rk_sum) * v


def _rwkv_prep(x, norm_w, w_r, mu, w0, a0, k_k, k_a, r_k, w2f, w2b, a2f, a2b, g2p, *, seq_len, tm):
    n, d = x.shape
    rdim = k_k.shape[1]
    consts = (norm_w, w_r, mu, w0, a0, k_k, k_a, r_k, w2f, w2b, a2f, a2b, g2p)
    return pl.pallas_call(
        functools.partial(_rwkv_prep_kernel, tiles_per_seq=seq_len // tm, rdim=rdim),
        grid=(n // tm,),
        in_specs=_halo_specs(tm, d, n) + [_resident(a) for a in consts],
        out_specs=[pl.BlockSpec((tm, rdim), lambda i: (i, 0))] * 11,
        out_shape=[jax.ShapeDtypeStruct((n, rdim), F32)] * 11,
        compiler_params=_params(("parallel",)),
        name="rwkv_prep",
    )(x, x, x, *consts)


def _block_diag(y, head0):
    return jnp.concatenate([jnp.where(head0, y, 0.0), jnp.where(head0, 0.0, y)], axis=0)


def _unit_triangular_inverse(a_list):
    c = CHUNK
    rng = range(len(a_list))
    row = lax.broadcasted_iota(jnp.int32, (c, LANES), 0)
    lane = lax.broadcasted_iota(jnp.int32, (c, LANES), 1)
    head0 = lane < HEAD_SIZE
    eye = (lane % HEAD_SIZE == row).astype(F32)
    bd = lambda y: _block_diag(y, head0)
    t = [eye + a for a in a_list]
    p = [_mm(a, bd(a)) for a in a_list]
    for _ in range(c.bit_length() - 3):
        both = [_mm(jnp.concatenate([t[i], p[i]], axis=0), bd(p[i])) for i in rng]
        t = [t[i] + both[i][:c] for i in rng]
        p = [both[i][c:] for i in rng]
    return [t[i] + _mm(t[i], bd(p[i])) for i in rng]


def _scan_intra(ops, masks):
    head0, strict, incl, eye = masks
    c = CHUNK
    rng = range(len(ops))
    at, bt, kt, rt, bh, kh, v, decay_c = (list(t) for t in zip(*ops))
    bd = lambda y: _block_diag(y, head0)

    a_all = [lax.dot_general(
        jnp.concatenate([at[i], rt[i]], axis=0).astype(BF16),
        jnp.concatenate([bd(bt[i]), bd(kt[i])], axis=0).astype(BF16),
        (((1,), (1,)), ((), ())), preferred_element_type=F32) for i in rng]
    a_ab = [jnp.where(strict, a_all[i][:c, :LANES], 0.0) for i in rng]
    a_ak = [jnp.where(strict, a_all[i][:c, LANES:], 0.0) for i in rng]
    a_rb = [jnp.where(incl, a_all[i][c:, :LANES], 0.0) for i in rng]
    a_rk = [jnp.where(incl, a_all[i][c:, LANES:], 0.0) for i in rng]

    t_inv = _unit_triangular_inverse(a_ab)

    xo = [_mm(jnp.concatenate([a_ak[i], a_rk[i]], axis=0), bd(v[i])) for i in rng]
    ap_w = [_mm(t_inv[i], jnp.concatenate([bd(at[i]), bd(xo[i][:c])], axis=1)) for i in rng]
    q_o = [_mm(a_rb[i], jnp.concatenate([bd(ap_w[i][:, :LANES]), bd(ap_w[i][:, LANES:])], axis=1)) for i in rng]
    lt = [jnp.transpose(jnp.concatenate([bh[i], kh[i]], axis=0)) for i in rng]
    zero = jnp.zeros((c, LANES), F32)
    mn = [_mm(lt[i], jnp.concatenate([ap_w[i], jnp.concatenate([zero, v[i]], axis=1)], axis=0))
          for i in rng]

    res = []
    for i in rng:
        qp = rt[i] + q_o[i][:, :LANES]
        m = jnp.where(head0, mn[i][:c, :LANES], mn[i][c:, :LANES]) + jnp.where(eye, decay_c[i], 0.0)
        n = jnp.where(head0, mn[i][:c, LANES:], mn[i][c:, LANES:])
        res.append((jnp.concatenate([qp, m], axis=0), q_o[i][:, LANES:] + xo[i][c:], n))
    return res


def _scan_kernel(r_ref, v_ref, kk_ref, kka_ref, kd_ref, ld_ref, o_ref, h_scr, *, reverse, pairs, chunks):
    c = CHUNK

    @pl.when(pl.program_id(1) == 0)
    def _():
        h_scr[...] = jnp.zeros_like(h_scr)

    t_i = lax.broadcasted_iota(jnp.int32, (c, c), 0)
    s_i = lax.broadcasted_iota(jnp.int32, (c, c), 1)
    tri = ((s_i >= t_i) if reverse else (s_i <= t_i)).astype(BF16)
    tp = lax.broadcasted_iota(jnp.int32, (c, LANES), 0)
    lane = lax.broadcasted_iota(jnp.int32, (c, LANES), 1)
    sp = lane % HEAD_SIZE
    head0 = lane < HEAD_SIZE
    strict = (sp > tp) if reverse else (sp < tp)
    incl = (sp >= tp) if reverse else (sp <= tp)
    masks = (head0, strict, incl, sp == tp)
    last = 0 if reverse else c - 1

    group = SCAN_GROUP if chunks % SCAN_GROUP == 0 else 1

    def chunk_operands(rows):
        ld = ld_ref[rows, :]
        cum = sum(jnp.dot(tri, t, preferred_element_type=F32) for t in _split3(ld))
        cum_c = cum[last:last + 1, :]
        e_neg = jnp.exp(-cum)
        e_tail = jnp.exp(cum_c - cum)
        kka = kka_ref[rows, :]
        kd = kd_ref[rows, :]
        full = (-kk_ref[rows, :] * jnp.exp(cum - ld), kka * e_neg, kd * e_neg, r_ref[rows, :] * jnp.exp(cum),
                kka * e_tail, kd * e_tail, v_ref[rows, :], jnp.exp(cum_c))
        return [tuple(t[:, p * LANES:(p + 1) * LANES] for t in full) for p in range(pairs)]

    def group_body(gi, carry):
        rows = []
        for j in range(group):
            ci = gi * group + j
            cc = (chunks - 1 - ci) if reverse else ci
            rows.append(pl.ds(pl.multiple_of(cc * c, c), c))
        ops = [op for rw in rows for op in chunk_operands(rw)]
        intra = _scan_intra(ops, masks)
        h = [h_scr[p] for p in range(pairs)]
        for j in range(group):
            for p in range(pairs):
                lhs, o_intra, n = intra[j * pairs + p]
                oh = _mm(lhs, _block_diag(h[p], head0))
                o_ref[rows[j], p * LANES:(p + 1) * LANES] = oh[:c] + o_intra
                h[p] = oh[c:] + n
        for p in range(pairs):
            h_scr[p] = h[p]
        return carry

    lax.fori_loop(0, chunks // group, group_body, 0)


def _scan(r, v, kk, kka, kd, ld, *, reverse, seq_len, tb):
    n, rdim = r.shape
    n_seq = n // seq_len
    bps = seq_len // tb
    pairs = rdim // LANES

    def idx(s, j):
        jj = (bps - 1 - j) if reverse else j
        return (s * bps + jj, 0)

    spec = pl.BlockSpec((tb, rdim), idx)
    return pl.pallas_call(
        functools.partial(_scan_kernel, reverse=reverse, pairs=pairs, chunks=tb // CHUNK),
        grid=(n_seq, bps),
        in_specs=[spec] * 6,
        out_specs=spec,
        out_shape=jax.ShapeDtypeStruct((n, rdim), F32),
        scratch_shapes=[pltpu.VMEM((pairs, CHUNK, LANES), F32)],
        compiler_params=_params(("parallel", "arbitrary")),
        name="scan_bwd" if reverse else "scan_fwd",
    )(r, v, kk, kka, kd, ld)


def _post_kernel(of_ref, ob_ref, bonus_ref, g_ref, yc_ref, x_ref, lnw_ref, lnb_ref, wc_ref, wr_ref, o_ref):
    o = of_ref[...] + ob_ref[...]
    mean = _head_sum(o) * (1.0 / HEAD_SIZE)
    d = o - mean
    var = _head_sum(d * d) * (1.0 / HEAD_SIZE)
    y = d * lax.rsqrt(var + GN_EPS) * lnw_ref[...] + lnb_ref[...] + bonus_ref[...]
    y = y * g_ref[...]
    o_ref[...] = x_ref[...] + _mm(yc_ref[...], wc_ref[...]) + _mm(y, wr_ref[...])


def _post(o_f, o_b, bonus, g, yc, x, ln_w, ln_b, w_conv, w_rwkv, *, tm):
    n, d = x.shape
    cdim = yc.shape[1]
    rdim = o_f.shape[1]
    rspec = pl.BlockSpec((tm, rdim), lambda i: (i, 0))
    return pl.pallas_call(
        _post_kernel,
        grid=(n // tm,),
        in_specs=[rspec, rspec, rspec, rspec, pl.BlockSpec((tm, cdim), lambda i: (i, 0)),
                  pl.BlockSpec((tm, d), lambda i: (i, 0)),
                  _resident(ln_w), _resident(ln_b), _resident(w_conv), _resident(w_rwkv)],
        out_specs=pl.BlockSpec((tm, d), lambda i: (i, 0)),
        out_shape=jax.ShapeDtypeStruct((n, d), F32),
        compiler_params=_params(("parallel",)),
        name="post",
    )(o_f, o_b, bonus, g, yc, x, ln_w, ln_b, w_conv, w_rwkv)


def _tile(n, pref):
    t = min(pref, n)
    assert n % t == 0, (n, t)
    return t


def _pad_rows(w, rows_before, rows_total):
    return jnp.pad(w, ((rows_before, rows_total - rows_before - w.shape[0]), (0, 0)))


ROW_TILES = {"ffn": 1024, "conv_mix": 512, "rwkv_prep": 256, "scan": 512, "post": 512}
FFN_HIDDEN_TILE = 512


def _layer(x, seq_len, p):
    tiles = {name: _tile(seq_len, rows) for name, rows in ROW_TILES.items()}
    tf = _tile(p["ffn1_wg"].shape[1], FFN_HIDDEN_TILE)
    x1 = _ffn(x, p["ffn1_norm"], p["ffn1_wg"], p["ffn1_wu"], p["ffn1_wd"], p["final_norm"],
              final=False, tm=tiles["ffn"], tf=tf)
    yc = _conv_mix(x1, p["mix_norm"], p["w_in_c"], p["conv_w"], seq_len=seq_len, tm=tiles["conv_mix"])
    (r, v, kk, kka_f, kka_b, k_f, k_b, ld_f, ld_b, bonus, g) = _rwkv_prep(
        x1, p["mix_norm"], p["w_in_r"], p["mu"], p["w0"], p["a0"], p["k_k"], p["k_a"], p["r_k"],
        p["w2f"], p["w2b"], p["a2f"], p["a2b"], p["g2"], seq_len=seq_len, tm=tiles["rwkv_prep"])
    o_f = _scan(r, v, kk, kka_f, k_f, ld_f, reverse=False, seq_len=seq_len, tb=tiles["scan"])
    o_b = _scan(r, v, kk, kka_b, k_b, ld_b, reverse=True, seq_len=seq_len, tb=tiles["scan"])
    x2 = _post(o_f, o_b, bonus, g, yc, x1, p["ln_w"], p["ln_b"], p["w_out_c"], p["w_out_r"], tm=tiles["post"])
    return _ffn(x2, p["ffn2_norm"], p["ffn2_wg"], p["ffn2_wu"], p["ffn2_wd"], p["final_norm"],
                final=True, tm=tiles["ffn"], tf=tf)


def kernel(x_prompt, x_sample, ffn1_norm, ffn1_w_gate, ffn1_w_up, ffn1_w_down, mix_norm, w_in, conv_w, mu_shift,
           w0, w2, a0, a2, g2, k_k, k_a, r_k, ln_x_w, ln_x_b, w_out, ffn2_norm, ffn2_w_gate, ffn2_w_up,
           ffn2_w_down, final_norm):
    assert ffn1_norm.shape[0] == 1, "single-layer trunk"
    d = x_prompt.shape[-1]
    cdim = conv_w.shape[-1]
    rdim = k_k.shape[-1]
    in_cols = w_in.shape[-1]
    rwkv_cols = in_cols - 3 * cdim
    assert rwkv_cols == 3 * rdim + LORA_COLS + GATE_LORA
    pad_cols = 3 * rdim + LORA_COLS + GATE_PAD - rwkv_cols
    row = lambda a: a.reshape(1, -1).astype(F32)
    bf = lambda a: a.astype(BF16)
    p = {
        "ffn1_norm": row(ffn1_norm[0]), "ffn1_wg": bf(ffn1_w_gate[0]), "ffn1_wu": bf(ffn1_w_up[0]),
        "ffn1_wd": bf(ffn1_w_down[0]),
        "ffn2_norm": row(ffn2_norm[0]), "ffn2_wg": bf(ffn2_w_gate[0]), "ffn2_wu": bf(ffn2_w_up[0]),
        "ffn2_wd": bf(ffn2_w_down[0]),
        "final_norm": row(final_norm), "mix_norm": row(mix_norm[0]),
        "w_in_c": bf(w_in[0, :, :3 * cdim]),
        "w_in_r": bf(jnp.pad(w_in[0, :, 3 * cdim:], ((0, 0), (0, pad_cols)))),
        "mu": jnp.pad(mu_shift[0], (0, pad_cols)).reshape(1, -1),
        "conv_w": conv_w[0], "w0": w0[0], "a0": a0[0],
        "k_k": row(k_k[0]), "k_a": row(k_a[0]), "r_k": row(r_k[0]),
        "w2f": bf(_pad_rows(w2[0, 0], 0, 2 * DECAY_LORA)), "w2b": bf(_pad_rows(w2[0, 1], DECAY_LORA, 2 * DECAY_LORA)),
        "a2f": bf(_pad_rows(a2[0, 0], 0, 2 * AAA_LORA)), "a2b": bf(_pad_rows(a2[0, 1], AAA_LORA, 2 * AAA_LORA)),
        "g2": bf(_pad_rows(g2[0], 0, GATE_PAD)),
        "ln_w": row(ln_x_w[0]), "ln_b": row(ln_x_b[0]),
        "w_out_c": bf(w_out[0, :cdim]), "w_out_r": bf(w_out[0, cdim:]),
    }
    outs = []
    for x in (x_prompt, x_sample):
        b, t, _ = x.shape
        outs.append(_layer(x.reshape(b * t, d), t, p).reshape(b, t, d))
    return tuple(outs)
```

```python
import functools

import jax
import jax.numpy as jnp
from jax import lax
from jax.experimental import pallas as pl
from jax.experimental.pallas import tpu as pltpu

F32 = jnp.float32
BF16 = jnp.bfloat16

HEAD_SIZE = 64
LANES = 128
MXU_WIDTH = 256
SUBLANES = 8
CHUNK = 64
FFN_SLAB = 128
FFN_ROWS = 512
SCAN_GROUP = 2
DECAY_LORA = 64
AAA_LORA = 64
GATE_LORA = 160
LORA_COLS = 2 * DECAY_LORA + 2 * AAA_LORA
GATE_PAD = 256
FFN_RESIDUAL_SCALE = 0.5
RMS_EPS = 1e-6
GN_EPS = 64e-5
L2_EPS = 1e-12
DECAY_SCALE = 0.6065306597126334
VMEM_LIMIT = 60 * 1024 * 1024


def _params(sem):
    return pltpu.CompilerParams(dimension_semantics=sem, vmem_limit_bytes=VMEM_LIMIT)


def _rms(x, w):
    return x * lax.rsqrt(jnp.mean(x * x, axis=-1, keepdims=True) + RMS_EPS) * w


def _sigmoid(x):
    return 1.0 / (1.0 + jnp.exp(-x))


def _mm(a, b):
    return jnp.dot(a.astype(BF16), b.astype(BF16), preferred_element_type=F32)


def _ffn_kernel(x_ref, nw_ref, wg_ref, wu_ref, wd_ref, fw_ref, o_ref, h_scr, *, final):
    j = pl.program_id(1)

    slabs = [pl.ds(r, FFN_SLAB) for r in range(0, x_ref.shape[0], FFN_SLAB)]

    @pl.when(j == 0)
    def _():
        for rows in slabs:
            h_scr[rows, :] = _rms(x_ref[rows, :], nw_ref[...]).astype(BF16)
        o_ref[...] = jnp.zeros_like(o_ref)

    half = min(FFN_ROWS, x_ref.shape[0])
    for r in range(0, x_ref.shape[0], half):
        rows = pl.ds(r, half)
        h = h_scr[rows, :]
        g = jnp.dot(h, wg_ref[...], preferred_element_type=F32)
        u = jnp.dot(h, wu_ref[...], preferred_element_type=F32)
        a = ((g * _sigmoid(g)) * u).astype(BF16)
        o_ref[rows, :] += jnp.dot(a, wd_ref[...], preferred_element_type=F32)

    @pl.when(j == pl.num_programs(1) - 1)
    def _():
        for rows in slabs:
            y = x_ref[rows, :] + FFN_RESIDUAL_SCALE * o_ref[rows, :]
            if final:
                y = _rms(y, fw_ref[...])
            o_ref[rows, :] = y


def _ffn(x, norm_w, wg, wu, wd, final_w, *, final, tm, tf):
    n, d = x.shape
    dff = wg.shape[1]
    return pl.pallas_call(
        functools.partial(_ffn_kernel, final=final),
        grid=(n // tm, dff // tf),
        in_specs=[
            pl.BlockSpec((tm, d), lambda i, j: (i, 0)),
            pl.BlockSpec((1, d), lambda i, j: (0, 0)),
            pl.BlockSpec((d, tf), lambda i, j: (0, j)),
            pl.BlockSpec((d, tf), lambda i, j: (0, j)),
            pl.BlockSpec((tf, d), lambda i, j: (j, 0)),
            pl.BlockSpec((1, d), lambda i, j: (0, 0)),
        ],
        out_specs=pl.BlockSpec((tm, d), lambda i, j: (i, 0)),
        out_shape=jax.ShapeDtypeStruct((n, d), F32),
        scratch_shapes=[pltpu.VMEM((tm, d), BF16)],
        compiler_params=_params(("parallel", "arbitrary")),
        name="ffn_final" if final else "ffn",
    )(x, norm_w, wg, wu, wd, final_w)


def _split3(x):
    hi = x.astype(BF16)
    rest = x - hi.astype(F32)
    mid = rest.astype(BF16)
    lo = (rest - mid.astype(F32)).astype(BF16)
    return hi, mid, lo


def _head_ones(width):
    r = lax.broadcasted_iota(jnp.int32, (width, width), 0) // HEAD_SIZE
    c = lax.broadcasted_iota(jnp.int32, (width, width), 1) // HEAD_SIZE
    return (r == c).astype(BF16)


def _head_sum(x):
    width = MXU_WIDTH if x.shape[1] % MXU_WIDTH == 0 else LANES
    ones_bd = _head_ones(width)
    parts = []
    for lo in range(0, x.shape[1], width):
        hi_mid_lo = _split3(x[:, lo:lo + width])
        parts.append(sum(jnp.dot(t, ones_bd, preferred_element_type=F32) for t in hi_mid_lo))
    return parts[0] if len(parts) == 1 else jnp.concatenate(parts, axis=1)


def _shift_rows(x, prev_row, next_row):
    rows = x.shape[0]
    ridx = lax.broadcasted_iota(jnp.int32, x.shape, 0)
    dn = jnp.where(ridx == 0, prev_row, pltpu.roll(x, 1, axis=0))
    up = jnp.where(ridx == rows - 1, next_row, pltpu.roll(x, rows - 1, axis=0))
    return dn, up


def _halo_specs(tm, d, n):
    sub_per_tile = tm // SUBLANES
    last_sub = n // SUBLANES - 1
    return [
        pl.BlockSpec((tm, d), lambda i: (i, 0)),
        pl.BlockSpec((SUBLANES, d), lambda i: (jnp.maximum(i * sub_per_tile - 1, 0), 0)),
        pl.BlockSpec((SUBLANES, d), lambda i: (jnp.minimum((i + 1) * sub_per_tile, last_sub), 0)),
    ]


def _resident(a):
    return pl.BlockSpec(a.shape, lambda i: (0,) * a.ndim, pipeline_mode=pl.Buffered(1))


def _normed_with_halo(x_ref, xp_ref, xn_ref, nw_ref):
    x_all = jnp.concatenate([x_ref[...], xp_ref[...], xn_ref[...]], axis=0)
    return _rms(x_all, nw_ref[...]).astype(BF16)


def _halo_flags(tiles_per_seq):
    i = pl.program_id(0)
    has_prev = (i % tiles_per_seq != 0).astype(F32)
    has_next = (i % tiles_per_seq != tiles_per_seq - 1).astype(F32)
    return has_prev, has_next


def _split_halo(z, has_prev, has_next):
    tm = z.shape[0] - 2 * SUBLANES
    prev_at = tm + SUBLANES - 1
    next_at = tm + SUBLANES
    return z[:tm], z[prev_at:prev_at + 1] * has_prev, z[next_at:next_at + 1] * has_next


def _conv_kernel(x_ref, xp_ref, xn_ref, nw_ref, w_ref, cw_ref, yc_ref, *, tiles_per_seq, cdim):
    has_prev, has_next = _halo_flags(tiles_per_seq)
    h = _normed_with_halo(x_ref, xp_ref, xn_ref, nw_ref)
    tm = x_ref.shape[0]
    proj = lambda rows, lo: jnp.dot(rows, w_ref[:, lo:lo + cdim], preferred_element_type=F32)
    b_gate = proj(h[:tm], 0)
    u, u_prev, u_next = _split_halo(proj(h, cdim) * proj(h, 2 * cdim), has_prev, has_next)
    u_dn, u_up = _shift_rows(u, u_prev, u_next)
    yc_ref[...] = b_gate * (cw_ref[0:1, :] * u_dn + cw_ref[1:2, :] * u + cw_ref[2:3, :] * u_up)


def _conv_mix(x, norm_w, w_c, conv_w, *, seq_len, tm):
    n, d = x.shape
    cdim = conv_w.shape[1]
    return pl.pallas_call(
        functools.partial(_conv_kernel, tiles_per_seq=seq_len // tm, cdim=cdim),
        grid=(n // tm,),
        in_specs=_halo_specs(tm, d, n) + [_resident(norm_w), _resident(w_c), _resident(conv_w)],
        out_specs=pl.BlockSpec((tm, cdim), lambda i: (i, 0)),
        out_shape=jax.ShapeDtypeStruct((n, cdim), F32),
        compiler_params=_params(("parallel",)),
        name="conv_mix",
    )(x, x, x, norm_w, w_c, conv_w)


def _rwkv_prep_kernel(x_ref, xp_ref, xn_ref, nw_ref, w_ref, mu_ref, w0_ref, a0_ref, kk_ref, ka_ref, rk_ref,
                      w2f_ref, w2b_ref, a2f_ref, a2b_ref, g2_ref,
                      r_ref, v_ref, kkn_ref, kkaf_ref, kkab_ref, kf_ref, kb_ref,
                      ldf_ref, ldb_ref, bonus_ref, g_ref, *, tiles_per_seq, rdim):
    has_prev, has_next = _halo_flags(tiles_per_seq)
    h = _normed_with_halo(x_ref, xp_ref, xn_ref, nw_ref)

    def shifted(lo, hi):
        z = jnp.dot(h, w_ref[:, lo:hi], preferred_element_type=F32)
        zc, z_prev, z_next = _split_halo(z, has_prev, has_next)
        dn, up = _shift_rows(zc, z_prev, z_next)
        return zc + mu_ref[:, lo:hi] * (0.5 * (dn + up) - zc)

    zl = shifted(3 * rdim, 3 * rdim + LORA_COLS)
    zg = shifted(3 * rdim + LORA_COLS, 3 * rdim + LORA_COLS + GATE_PAD)
    g_ref[...] = _mm(_sigmoid(zg), g2_ref[...])
    th = jnp.tanh(zl[:, 0:2 * DECAY_LORA])
    za = zl[:, 2 * DECAY_LORA:LORA_COLS]
    a_dirs = []
    for d, (w2_ref, a2_ref, ld_ref) in enumerate(((w2f_ref, a2f_ref, ldf_ref), (w2b_ref, a2b_ref, ldb_ref))):
        y = w0_ref[d:d + 1, :] + _mm(th, w2_ref[...])
        ld_ref[...] = -DECAY_SCALE * _sigmoid(y)
        a_dirs.append(_sigmoid(a0_ref[d:d + 1, :] + _mm(za, a2_ref[...])))

    k = shifted(rdim, 2 * rdim)
    kk = k * kk_ref[...]
    nrm = jnp.sqrt(_head_sum(kk * kk))
    kkn = kk / jnp.maximum(nrm, L2_EPS)
    kkn_ref[...] = kkn
    r = shifted(0, rdim)
    r_ref[...] = r
    rk_sum = jnp.zeros_like(r)
    for a, kka_ref, kd_ref in zip(a_dirs, (kkaf_ref, kkab_ref), (kf_ref, kb_ref)):
        k_d = k * (1.0 + (a - 1.0) * ka_ref[...])
        kka_ref[...] = kkn * a
        kd_ref[...] = k_d
        rk_sum = rk_sum + r * k_d * rk_ref[...]
    v = shifted(2 * rdim, 3 * rdim)
    v_ref[...] = v
    bonus_ref[...] = _head_sum(rk_sum) * v


def _rwkv_prep(x, norm_w, w_r, mu, w0, a0, k_k, k_a, r_k, w2f, w2b, a2f, a2b, g2p, *, seq_len, tm):
    n, d = x.shape
    rdim = k_k.shape[1]
    consts = (norm_w, w_r, mu, w0, a0, k_k, k_a, r_k, w2f, w2b, a2f, a2b, g2p)
    return pl.pallas_call(
        functools.partial(_rwkv_prep_kernel, tiles_per_seq=seq_len // tm, rdim=rdim),
        grid=(n // tm,),
        in_specs=_halo_specs(tm, d, n) + [_resident(a) for a in consts],
        out_specs=[pl.BlockSpec((tm, rdim), lambda i: (i, 0))] * 11,
        out_shape=[jax.ShapeDtypeStruct((n, rdim), F32)] * 11,
        compiler_params=_params(("parallel",)),
        name="rwkv_prep",
    )(x, x, x, *consts)


def _block_diag(y, head0):
    return jnp.concatenate([jnp.where(head0, y, 0.0), jnp.where(head0, 0.0, y)], axis=0)


def _unit_triangular_inverse(a_list):
    c = CHUNK
    rng = range(len(a_list))
    row = lax.broadcasted_iota(jnp.int32, (c, LANES), 0)
    lane = lax.broadcasted_iota(jnp.int32, (c, LANES), 1)
    head0 = lane < HEAD_SIZE
    eye = (lane % HEAD_SIZE == row).astype(F32)
    bd = lambda y: _block_diag(y, head0)
    t = [eye + a for a in a_list]
    p = [_mm(a, bd(a)) for a in a_list]
    for _ in range(c.bit_length() - 3):
        both = [_mm(jnp.concatenate([t[i], p[i]], axis=0), bd(p[i])) for i in rng]
        t = [t[i] + both[i][:c] for i in rng]
        p = [both[i][c:] for i in rng]
    return [t[i] + _mm(t[i], bd(p[i])) for i in rng]


def _scan_intra(ops, masks, after_first_stage):
    head0, strict, incl, eye = masks
    c = CHUNK
    rng = range(len(ops))
    at, bt, kt, rt, bh, kh, v, decay_c = (list(t) for t in zip(*ops))
    bd = lambda y: _block_diag(y, head0)

    a_all = [lax.dot_general(
        jnp.concatenate([at[i], rt[i]], axis=0).astype(BF16),
        jnp.concatenate([bd(bt[i]), bd(kt[i])], axis=0).astype(BF16),
        (((1,), (1,)), ((), ())), preferred_element_type=F32) for i in rng]
    after_first_stage()
    a_ab = [jnp.where(strict, a_all[i][:c, :LANES], 0.0) for i in rng]
    a_ak = [jnp.where(strict, a_all[i][:c, LANES:], 0.0) for i in rng]
    a_rb = [jnp.where(incl, a_all[i][c:, :LANES], 0.0) for i in rng]
    a_rk = [jnp.where(incl, a_all[i][c:, LANES:], 0.0) for i in rng]

    t_inv = _unit_triangular_inverse(a_ab)

    xo = [_mm(jnp.concatenate([a_ak[i], a_rk[i]], axis=0), bd(v[i])) for i in rng]
    ap_w = [_mm(t_inv[i], jnp.concatenate([bd(at[i]), bd(xo[i][:c])], axis=1)) for i in rng]
    q_o = [_mm(a_rb[i], jnp.concatenate([bd(ap_w[i][:, :LANES]), bd(ap_w[i][:, LANES:])], axis=1)) for i in rng]
    lt = [jnp.transpose(jnp.concatenate([bh[i], kh[i]], axis=0)) for i in rng]
    zero = jnp.zeros((c, LANES), F32)
    mn = [_mm(lt[i], jnp.concatenate([ap_w[i], jnp.concatenate([zero, v[i]], axis=1)], axis=0))
          for i in rng]

    res = []
    for i in rng:
        qp = rt[i] + q_o[i][:, :LANES]
        m = jnp.where(head0, mn[i][:c, :LANES], mn[i][c:, :LANES]) + jnp.where(eye, decay_c[i], 0.0)
        n = jnp.where(head0, mn[i][:c, LANES:], mn[i][c:, LANES:])
        res.append((jnp.concatenate([qp, m], axis=0), q_o[i][:, LANES:] + xo[i][c:], n))
    return res


def _scan_kernel(r_ref, v_ref, kk_ref, kka_ref, kd_ref, ld_ref, o_ref, h_scr, *, reverse, pairs, chunks):
    c = CHUNK

    @pl.when(pl.program_id(1) == 0)
    def _():
        h_scr[...] = jnp.zeros_like(h_scr)

    t_i = lax.broadcasted_iota(jnp.int32, (c, c), 0)
    s_i = lax.broadcasted_iota(jnp.int32, (c, c), 1)
    tri = ((s_i >= t_i) if reverse else (s_i <= t_i)).astype(BF16)
    tp = lax.broadcasted_iota(jnp.int32, (c, LANES), 0)
    lane = lax.broadcasted_iota(jnp.int32, (c, LANES), 1)
    sp = lane % HEAD_SIZE
    head0 = lane < HEAD_SIZE
    strict = (sp > tp) if reverse else (sp < tp)
    incl = (sp >= tp) if reverse else (sp <= tp)
    masks = (head0, strict, incl, sp == tp)
    last = 0 if reverse else c - 1

    group = SCAN_GROUP if chunks % SCAN_GROUP == 0 else 1

    def chunk_operands(rows):
        ld = ld_ref[rows, :]
        cum = sum(jnp.dot(tri, t, preferred_element_type=F32) for t in _split3(ld))
        cum_c = cum[last:last + 1, :]
        e_neg = jnp.exp(-cum)
        e_tail = jnp.exp(cum_c - cum)
        kka = kka_ref[rows, :]
        kd = kd_ref[rows, :]
        full = (-kk_ref[rows, :] * jnp.exp(cum - ld), kka * e_neg, kd * e_neg, r_ref[rows, :] * jnp.exp(cum),
                kka * e_tail, kd * e_tail, v_ref[rows, :], jnp.exp(cum_c))
        return [tuple(t[:, p * LANES:(p + 1) * LANES] for t in full) for p in range(pairs)]

    def state_stage(rows, results, h):
        for p in range(pairs):
            lhs, o_intra, n = results[p]
            oh = _mm(lhs, _block_diag(h[p], head0))
            o_ref[rows, p * LANES:(p + 1) * LANES] = oh[:c] + o_intra
            h[p] = oh[c:] + n

    n_groups = chunks // group

    def group_rows(gi):
        order = [gi * group + j for j in range(group)]
        return [pl.ds(((chunks - 1 - ci) if reverse else ci) * c, c) for ci in order]

    def group_operands(gi):
        return [op for rw in group_rows(gi) for op in chunk_operands(rw)]

    h = [h_scr[p] for p in range(pairs)]
    pending = []
    ready = {0: group_operands(0)}
    for gi in range(n_groups):
        if pending:
            state_stage(*pending.pop(0), h)

        def between_stages(gi=gi):
            while pending:
                state_stage(*pending.pop(0), h)
            if gi + 1 < n_groups:
                ready[gi + 1] = group_operands(gi + 1)

        intra = _scan_intra(ready.pop(gi), masks, between_stages)
        rows = group_rows(gi)
        pending.extend((rows[j], intra[j * pairs:(j + 1) * pairs]) for j in range(group))
    while pending:
        state_stage(*pending.pop(0), h)
    for p in range(pairs):
        h_scr[p] = h[p]


def _scan(r, v, kk, kka, kd, ld, *, reverse, seq_len, tb):
    n, rdim = r.shape
    n_seq = n // seq_len
    bps = seq_len // tb
    pairs = rdim // LANES

    def idx(s, j):
        jj = (bps - 1 - j) if reverse else j
        return (s * bps + jj, 0)

    spec = pl.BlockSpec((tb, rdim), idx)
    return pl.pallas_call(
        functools.partial(_scan_kernel, reverse=reverse, pairs=pairs, chunks=tb // CHUNK),
        grid=(n_seq, bps),
        in_specs=[spec] * 6,
        out_specs=spec,
        out_shape=jax.ShapeDtypeStruct((n, rdim), F32),
        scratch_shapes=[pltpu.VMEM((pairs, CHUNK, LANES), F32)],
        compiler_params=_params(("parallel", "arbitrary")),
        name="scan_bwd" if reverse else "scan_fwd",
    )(r, v, kk, kka, kd, ld)


def _post_kernel(of_ref, ob_ref, bonus_ref, g_ref, yc_ref, x_ref, lnw_ref, lnb_ref, wc_ref, wr_ref, o_ref):
    o = of_ref[...] + ob_ref[...]
    mean = _head_sum(o) * (1.0 / HEAD_SIZE)
    d = o - mean
    var = _head_sum(d * d) * (1.0 / HEAD_SIZE)
    y = d * lax.rsqrt(var + GN_EPS) * lnw_ref[...] + lnb_ref[...] + bonus_ref[...]
    y = y * g_ref[...]
    o_ref[...] = x_ref[...] + _mm(yc_ref[...], wc_ref[...]) + _mm(y, wr_ref[...])


def _post(o_f, o_b, bonus, g, yc, x, ln_w, ln_b, w_conv, w_rwkv, *, tm):
    n, d = x.shape
    cdim = yc.shape[1]
    rdim = o_f.shape[1]
    rspec = pl.BlockSpec((tm, rdim), lambda i: (i, 0))
    return pl.pallas_call(
        _post_kernel,
        grid=(n // tm,),
        in_specs=[rspec, rspec, rspec, rspec, pl.BlockSpec((tm, cdim), lambda i: (i, 0)),
                  pl.BlockSpec((tm, d), lambda i: (i, 0)),
                  _resident(ln_w), _resident(ln_b), _resident(w_conv), _resident(w_rwkv)],
        out_specs=pl.BlockSpec((tm, d), lambda i: (i, 0)),
        out_shape=jax.ShapeDtypeStruct((n, d), F32),
        compiler_params=_params(("parallel",)),
        name="post",
    )(o_f, o_b, bonus, g, yc, x, ln_w, ln_b, w_conv, w_rwkv)


def _tile(n, pref):
    t = min(pref, n)
    assert n % t == 0, (n, t)
    return t


def _pad_rows(w, rows_before, rows_total):
    return jnp.pad(w, ((rows_before, rows_total - rows_before - w.shape[0]), (0, 0)))


ROW_TILES = {"ffn": 1024, "conv_mix": 512, "rwkv_prep": 256, "scan": 512, "post": 512}
FFN_HIDDEN_TILE = 512


def _layer(x, seq_len, p):
    tiles = {name: _tile(seq_len, rows) for name, rows in ROW_TILES.items()}
    tf = _tile(p["ffn1_wg"].shape[1], FFN_HIDDEN_TILE)
    x1 = _ffn(x, p["ffn1_norm"], p["ffn1_wg"], p["ffn1_wu"], p["ffn1_wd"], p["final_norm"],
              final=False, tm=tiles["ffn"], tf=tf)
    yc = _conv_mix(x1, p["mix_norm"], p["w_in_c"], p["conv_w"], seq_len=seq_len, tm=tiles["conv_mix"])
    (r, v, kk, kka_f, kka_b, k_f, k_b, ld_f, ld_b, bonus, g) = _rwkv_prep(
        x1, p["mix_norm"], p["w_in_r"], p["mu"], p["w0"], p["a0"], p["k_k"], p["k_a"], p["r_k"],
        p["w2f"], p["w2b"], p["a2f"], p["a2b"], p["g2"], seq_len=seq_len, tm=tiles["rwkv_prep"])
    o_f = _scan(r, v, kk, kka_f, k_f, ld_f, reverse=False, seq_len=seq_len, tb=tiles["scan"])
    o_b = _scan(r, v, kk, kka_b, k_b, ld_b, reverse=True, seq_len=seq_len, tb=tiles["scan"])
    x2 = _post(o_f, o_b, bonus, g, yc, x1, p["ln_w"], p["ln_b"], p["w_out_c"], p["w_out_r"], tm=tiles["post"])
    return _ffn(x2, p["ffn2_norm"], p["ffn2_wg"], p["ffn2_wu"], p["ffn2_wd"], p["final_norm"],
                final=True, tm=tiles["ffn"], tf=tf)


def kernel(x_prompt, x_sample, ffn1_norm, ffn1_w_gate, ffn1_w_up, ffn1_w_down, mix_norm, w_in, conv_w, mu_shift,
           w0, w2, a0, a2, g2, k_k, k_a, r_k, ln_x_w, ln_x_b, w_out, ffn2_norm, ffn2_w_gate, ffn2_w_up,
           ffn2_w_down, final_norm):
    assert ffn1_norm.shape[0] == 1, "single-layer trunk"
    d = x_prompt.shape[-1]
    cdim = conv_w.shape[-1]
    rdim = k_k.shape[-1]
    in_cols = w_in.shape[-1]
    rwkv_cols = in_cols - 3 * cdim
    assert rwkv_cols == 3 * rdim + LORA_COLS + GATE_LORA
    pad_cols = 3 * rdim + LORA_COLS + GATE_PAD - rwkv_cols
    row = lambda a: a.reshape(1, -1).astype(F32)
    bf = lambda a: a.astype(BF16)
    p = {
        "ffn1_norm": row(ffn1_norm[0]), "ffn1_wg": bf(ffn1_w_gate[0]), "ffn1_wu": bf(ffn1_w_up[0]),
        "ffn1_wd": bf(ffn1_w_down[0]),
        "ffn2_norm": row(ffn2_norm[0]), "ffn2_wg": bf(ffn2_w_gate[0]), "ffn2_wu": bf(ffn2_w_up[0]),
        "ffn2_wd": bf(ffn2_w_down[0]),
        "final_norm": row(final_norm), "mix_norm": row(mix_norm[0]),
        "w_in_c": bf(w_in[0, :, :3 * cdim]),
        "w_in_r": bf(jnp.pad(w_in[0, :, 3 * cdim:], ((0, 0), (0, pad_cols)))),
        "mu": jnp.pad(mu_shift[0], (0, pad_cols)).reshape(1, -1),
        "conv_w": conv_w[0], "w0": w0[0], "a0": a0[0],
        "k_k": row(k_k[0]), "k_a": row(k_a[0]), "r_k": row(r_k[0]),
        "w2f": bf(_pad_rows(w2[0, 0], 0, 2 * DECAY_LORA)), "w2b": bf(_pad_rows(w2[0, 1], DECAY_LORA, 2 * DECAY_LORA)),
        "a2f": bf(_pad_rows(a2[0, 0], 0, 2 * AAA_LORA)), "a2b": bf(_pad_rows(a2[0, 1], AAA_LORA, 2 * AAA_LORA)),
        "g2": bf(_pad_rows(g2[0], 0, GATE_PAD)),
        "ln_w": row(ln_x_w[0]), "ln_b": row(ln_x_b[0]),
        "w_out_c": bf(w_out[0, :cdim]), "w_out_r": bf(w_out[0, cdim:]),
    }
    outs = []
    for x in (x_prompt, x_sample):
        b, t, _ = x.shape
        outs.append(_layer(x.reshape(b * t, d), t, p).reshape(b, t, d))
    return tuple(outs)
```
